```python
import math
import jax, jax.numpy as jnp
from jax import lax
import numpy as np

D_MODEL = 1024
BATCH = 8
SEQ = 2048
DEPTH = 1

PLE_DIM = 256
D_FF = 2816
CONV_CH = 512
CONV_K = 3
SSM_WIDTH = 512
SSM_GROUP = 16
SSM_GROUPS = SSM_WIDTH // SSM_GROUP
SSM_STATE = 64
ALPHA = (2.0 * DEPTH) ** 0.25
BETA = (8.0 * DEPTH) ** -0.25
LN_EPS = 1e-5
IN_COLS = 3 * CONV_CH + SSM_WIDTH + 2 * D_MODEL

kernel_name = "hybrid_conv_s5_macaron_deepnorm_block"


def layer_norm(x, g, b):
    xf = x.astype(jnp.float32)
    mu = jnp.mean(xf, axis=-1, keepdims=True)
    xc = xf - mu
    var = jnp.mean(xc * xc, axis=-1, keepdims=True)
    y = xc * lax.rsqrt(var + LN_EPS) * g.astype(jnp.float32) + b.astype(jnp.float32)
    return y.astype(x.dtype)


def swiglu(x, w_in, w_out):
    gate, up = jnp.split(x @ w_in, 2, axis=-1)
    return (jax.nn.silu(gate) * up) @ w_out


def causal_depthwise_conv(z, w, b):
    c = z.shape[-1]
    y = lax.conv_general_dilated(
        z, w[:, None, :].astype(z.dtype), window_strides=(1,),
        padding=[(CONV_K - 1, 0)], dimension_numbers=("NWC", "WIO", "NWC"),
        feature_group_count=c)
    return y + b


def s5_scan(u, lam_re, lam_im, log_step, b_re, b_im, c_re, c_im, d_skip):
    f32 = jnp.float32
    lam = lax.complex(lam_re.astype(f32), lam_im.astype(f32))
    dt = jnp.exp(log_step.astype(f32))[:, None]
    lam_bar = jnp.exp(lam * dt)
    b_c = lax.complex(b_re.astype(f32), b_im.astype(f32))
    c_c = lax.complex(c_re.astype(f32), c_im.astype(f32))
    b_bar = ((lam_bar - 1.0) / lam)[..., None] * b_c
    uf = u.astype(f32)
    bu = jnp.einsum("blgi,gni->blgn", uf.astype(jnp.complex64), b_bar)
    a = jnp.broadcast_to(lam_bar, bu.shape)

    def combine(left, right):
        a1, s1 = left
        a2, s2 = right
        return a1 * a2, a2 * s1 + s2

    _, states = lax.associative_scan(combine, (a, bu), axis=1)
    y = jnp.einsum("gin,blgn->blgi", c_c, states).real + d_skip.astype(f32) * uf
    bsz, seq = u.shape[0], u.shape[1]
    return y.reshape(bsz, seq, SSM_WIDTH).astype(u.dtype)


def token_mixer(h, w_in, conv_w, conv_b, conv_w_out, lam_re, lam_im, log_step,
                b_re, b_im, c_re, c_im, d_skip, w_glu, w_out):
    bsz, seq, _ = h.shape
    proj = h @ w_in
    cb, cc, ch, su, g_conv, g_ssm = jnp.split(
        proj, [CONV_CH, 2 * CONV_CH, 3 * CONV_CH, 3 * CONV_CH + SSM_WIDTH,
               3 * CONV_CH + SSM_WIDTH + D_MODEL], axis=-1)
    z = causal_depthwise_conv(cc * ch, conv_w, conv_b)
    y_conv = (cb * z) @ conv_w_out
    s = s5_scan(su.reshape(bsz, seq, SSM_GROUPS, SSM_GROUP), lam_re, lam_im, log_step,
                b_re, b_im, c_re, c_im, d_skip)
    s = jax.nn.gelu(s)
    ga, gb = jnp.split(s @ w_glu, 2, axis=-1)
    y_ssm = ga * jax.nn.sigmoid(gb)
    merged = jax.nn.sigmoid(g_conv) * y_conv + jax.nn.sigmoid(g_ssm) * y_ssm
    return merged @ w_out


def setup_inputs(seed: int = 0) -> dict:
    key = jax.random.key(seed)
    ks = jax.random.split(key, 40)
    f32 = jnp.float32
    nrm = lambda k, shape, s: (jax.random.normal(k, shape, f32) * s)
    L = DEPTH

    def gain(k):
        return 1.0 + nrm(k, (L, D_MODEL), 0.01)

    def bias(k, n=D_MODEL):
        return nrm(k, (L, n), 0.01)

    n_idx = jnp.arange(SSM_STATE, dtype=f32)
    lam_re = -0.5 + nrm(ks[20], (L, SSM_GROUPS, SSM_STATE), 0.01)
    lam_im = math.pi * n_idx[None, None, :] + nrm(ks[21], (L, SSM_GROUPS, SSM_STATE), 0.01)
    log_step = jax.random.uniform(ks[22], (L, SSM_GROUPS), f32,
                                  math.log(0.001), math.log(0.1))
    return {
        "x": jax.random.normal(ks[0], (BATCH, SEQ, D_MODEL), f32),
        "p": jax.random.normal(ks[1], (DEPTH, BATCH, SEQ, PLE_DIM), f32),
        "ffn1_w_in": nrm(ks[2], (L, D_MODEL, 2 * D_FF), D_MODEL ** -0.5),
        "ffn1_w_out": nrm(ks[3], (L, D_FF, D_MODEL), BETA * D_FF ** -0.5),
        "ln1_g": gain(ks[4]),
        "ln1_b": bias(ks[5]),
        "mix_w_in": nrm(ks[6], (L, D_MODEL, IN_COLS), D_MODEL ** -0.5),
        "conv_w": nrm(ks[7], (L, CONV_K, CONV_CH), CONV_K ** -0.5),
        "conv_b": bias(ks[8], CONV_CH),
        "conv_w_out": nrm(ks[9], (L, CONV_CH, D_MODEL), BETA * CONV_CH ** -0.5),
        "ssm_lam_re": lam_re,
        "ssm_lam_im": lam_im,
        "ssm_log_step": log_step,
        "ssm_b_re": nrm(ks[23], (L, SSM_GROUPS, SSM_STATE, SSM_GROUP), (2.0 * SSM_GROUP) ** -0.5),
        "ssm_b_im": nrm(ks[24], (L, SSM_GROUPS, SSM_STATE, SSM_GROUP), (2.0 * SSM_GROUP) ** -0.5),
        "ssm_c_re": nrm(ks[25], (L, SSM_GROUPS, SSM_GROUP, SSM_STATE), (2.0 * SSM_STATE) ** -0.5),
        "ssm_c_im": nrm(ks[26], (L, SSM_GROUPS, SSM_GROUP, SSM_STATE), (2.0 * SSM_STATE) ** -0.5),
        "ssm_d": nrm(ks[27], (L, SSM_GROUPS, SSM_GROUP), 1.0),
        "ssm_w_glu": nrm(ks[28], (L, SSM_WIDTH, 2 * D_MODEL), BETA * SSM_WIDTH ** -0.5),
        "mix_w_out": nrm(ks[29], (L, D_MODEL, D_MODEL), BETA * D_MODEL ** -0.5),
        "ln2_g": gain(ks[10]),
        "ln2_b": bias(ks[11]),
        "ffn2_w_in": nrm(ks[12], (L, D_MODEL, 2 * D_FF), D_MODEL ** -0.5),
        "ffn2_w_out": nrm(ks[13], (L, D_FF, D_MODEL), BETA * D_FF ** -0.5),
        "ln3_g": gain(ks[14]),
        "ln3_b": bias(ks[15]),
        "ple_w_in": nrm(ks[16], (L, PLE_DIM, D_MODEL), BETA * PLE_DIM ** -0.5),
        "ple_w_gate": nrm(ks[17], (L, D_MODEL, D_MODEL), D_MODEL ** -0.5),
        "ln4_g": gain(ks[18]),
        "ln4_b": bias(ks[19]),
    }


def reference(x, p, ffn1_w_in, ffn1_w_out, ln1_g, ln1_b, mix_w_in, conv_w, conv_b,
              conv_w_out, ssm_lam_re, ssm_lam_im, ssm_log_step, ssm_b_re, ssm_b_im,
              ssm_c_re, ssm_c_im, ssm_d, ssm_w_glu, mix_w_out, ln2_g, ln2_b,
              ffn2_w_in, ffn2_w_out, ln3_g, ln3_b, ple_w_in, ple_w_gate, ln4_g, ln4_b):
    for i in range(DEPTH):
        x = layer_norm(ALPHA * x + 0.5 * swiglu(x, ffn1_w_in[i], ffn1_w_out[i]),
                       ln1_g[i], ln1_b[i])
        mix = token_mixer(x, mix_w_in[i], conv_w[i], conv_b[i], conv_w_out[i],
                          ssm_lam_re[i], ssm_lam_im[i], ssm_log_step[i],
                          ssm_b_re[i], ssm_b_im[i], ssm_c_re[i], ssm_c_im[i], ssm_d[i],
                          ssm_w_glu[i], mix_w_out[i])
        x = layer_norm(ALPHA * x + mix, ln2_g[i], ln2_b[i])
        x = layer_norm(ALPHA * x + 0.5 * swiglu(x, ffn2_w_in[i], ffn2_w_out[i]),
                       ln3_g[i], ln3_b[i])
        e = (p[i] @ ple_w_in[i]) * jax.nn.sigmoid(x @ ple_w_gate[i])
        x = layer_norm(ALPHA * x + e, ln4_g[i], ln4_b[i])
    return x
```

```python
import functools
import math

import jax
import jax.numpy as jnp
from jax import lax
from jax.experimental import pallas as pl
from jax.experimental.pallas import tpu as pltpu

D_MODEL = 1024
BATCH = 8
SEQ = 2048
PLE_DIM = 256
D_FF = 2816
CONV_CH = 512
CONV_K = 3
SSM_WIDTH = 512
SSM_GROUP = 16
SSM_GROUPS = SSM_WIDTH // SSM_GROUP
SSM_STATE = 64
LN_EPS = 1e-5

SUBLANES = 8
MXU_DIM = 256
VMEM_LIMIT_BYTES = 56 * 1024 * 1024

ROWS = BATCH * SEQ
TM_FFN = 512
TM_MIX = 512
FF_CHUNKS = ((0, 768), (768, 768), (1536, 768), (2304, 512))
HALF_GROUPS = SSM_GROUPS // 2
HALF_IN = HALF_GROUPS * SSM_GROUP
HALF_ST = HALF_GROUPS * SSM_STATE
CONV_HALO = (CONV_K - 1) * BATCH

BF16 = jnp.bfloat16
F32 = jnp.float32


def _dot(a, b):
    return jnp.dot(a, b, preferred_element_type=F32)


def _layer_norm(y, g, b):
    mu = jnp.mean(y, axis=-1, keepdims=True)
    yc = y - mu
    var = jnp.mean(yc * yc, axis=-1, keepdims=True)
    return yc * lax.rsqrt(var + LN_EPS) * g + b


def _swiglu(xb, wg_ref, wu_ref, wo_ref):
    acc = None
    for c0, cw in FF_CHUNKS:
        gate = _dot(xb, wg_ref[:, c0:c0 + cw])
        up = _dot(xb, wu_ref[:, c0:c0 + cw])
        h = (gate * jax.nn.sigmoid(gate) * up).astype(BF16)
        part = _dot(h, wo_ref[c0:c0 + cw, :])
        acc = part if acc is None else acc + part
    return acc


def _ffn_ln_kernel(alpha, x_ref, wg_ref, wu_ref, wo_ref, g_ref, b_ref, o_ref):
    x = x_ref[...]
    f = _swiglu(x.astype(BF16), wg_ref, wu_ref, wo_ref)
    o_ref[...] = _layer_norm(alpha * x + 0.5 * f, g_ref[...], b_ref[...])


def _ffn_ple_kernel(alpha, x_ref, p_ref, wg_ref, wu_ref, wo_ref, g3_ref, b3_ref,
                    wpi_ref, wpg_ref, g4_ref, b4_ref, o_ref):
    x = x_ref[...]
    f = _swiglu(x.astype(BF16), wg_ref, wu_ref, wo_ref)
    x3 = _layer_norm(alpha * x + 0.5 * f, g3_ref[...], b3_ref[...])
    e = _dot(p_ref[...].astype(BF16), wpi_ref[...]) * jax.nn.sigmoid(
        _dot(x3.astype(BF16), wpg_ref[...]))
    o_ref[...] = _layer_norm(alpha * x3 + e, g4_ref[...], b4_ref[...])


def _mixer_kernel(alpha, x_ref, win_ref, cw_ref, cb_ref, cwo_ref, lam_ref, bd_ref,
                  cd_ref, dsk_ref, wglu_ref, wout_ref, g_ref, b_ref, o_ref,
                  s_ref, st_ref, v_ref):
    tm = x_ref.shape[0]
    step = pl.program_id(0)

    @pl.when(step == 0)
    def _():
        st_ref[...] = jnp.zeros_like(st_ref)
        v_ref[0:CONV_HALO, :] = jnp.zeros((CONV_HALO, CONV_CH), F32)

    @pl.when(step > 0)
    def _():
        v_ref[0:CONV_HALO, :] = v_ref[tm:tm + CONV_HALO, :]

    x = x_ref[...]
    xb = x.astype(BF16)

    def proj(c0, cw):
        return _dot(xb, win_ref[:, c0:c0 + cw])

    v_ref[CONV_HALO:CONV_HALO + tm, :] = proj(CONV_CH, CONV_CH) * proj(2 * CONV_CH, CONV_CH)
    z = cb_ref[...]
    for k in range(CONV_K):
        z = z + cw_ref[k:k + 1, :] * v_ref[k * BATCH:k * BATCH + tm, :]
    y_conv = _dot((proj(0, CONV_CH) * z).astype(BF16), cwo_ref[...])
    merged = jax.nn.sigmoid(proj(4 * CONV_CH, D_MODEL)) * y_conv

    u = proj(3 * CONV_CH, SSM_WIDTH)
    ub = u.astype(BF16)
    for h in range(2):
        s_ref[:, 2 * h * HALF_ST:2 * (h + 1) * HALF_ST] = _dot(
            ub[:, h * HALF_IN:(h + 1) * HALF_IN], bd_ref[h])

    for h in range(2):
        re0 = 2 * h * HALF_ST
        im0 = re0 + HALF_ST
        lam_re = lam_ref[:, re0:re0 + HALF_ST]
        lam_im = lam_ref[:, im0:im0 + HALF_ST]

        def body(l, carry, re0=re0, im0=im0, lam_re=lam_re, lam_im=lam_im):
            sr, si = carry
            r0 = pl.multiple_of(l * SUBLANES, SUBLANES)
            nr = lam_re * sr - lam_im * si + s_ref[pl.ds(r0, SUBLANES), re0:re0 + HALF_ST]
            ni = lam_re * si + lam_im * sr + s_ref[pl.ds(r0, SUBLANES), im0:im0 + HALF_ST]
            s_ref[pl.ds(r0, SUBLANES), re0:re0 + HALF_ST] = nr
            s_ref[pl.ds(r0, SUBLANES), im0:im0 + HALF_ST] = ni
            return nr, ni

        sr, si = lax.fori_loop(
            0, tm // SUBLANES, body,
            (st_ref[:, re0:re0 + HALF_ST], st_ref[:, im0:im0 + HALF_ST]), unroll=4)
        st_ref[:, re0:re0 + HALF_ST] = sr
        st_ref[:, im0:im0 + HALF_ST] = si

    ys = []
    for h in range(2):
        sb = s_ref[:, 2 * h * HALF_ST:2 * (h + 1) * HALF_ST].astype(BF16)
        ys.append(_dot(sb, cd_ref[h]))
    y = jnp.concatenate(ys, axis=-1) + dsk_ref[...] * u
    sg = jax.nn.gelu(y).astype(BF16)
    y_ssm = _dot(sg, wglu_ref[:, 0:D_MODEL]) * jax.nn.sigmoid(
        _dot(sg, wglu_ref[:, D_MODEL:2 * D_MODEL]))
    merged = merged + jax.nn.sigmoid(proj(4 * CONV_CH + D_MODEL, D_MODEL)) * y_ssm

    mix = _dot(merged.astype(BF16), wout_ref[...])
    o_ref[...] = _layer_norm(alpha * x + mix, g_ref[...], b_ref[...])


def _resident(shape):
    zeros = (0,) * len(shape)
    return pl.BlockSpec(shape, lambda i: zeros, pipeline_mode=pl.Buffered(1))


def _rows(tm, width):
    return pl.BlockSpec((tm, width), lambda i: (i, 0))


def _ffn_ln(x, wg, wu, wo, g, b, alpha):
    return pl.pallas_call(
        functools.partial(_ffn_ln_kernel, alpha),
        grid=(ROWS // TM_FFN,),
        in_specs=[_rows(TM_FFN, D_MODEL), _resident(wg.shape), _resident(wu.shape),
                  _resident(wo.shape), _resident(g.shape), _resident(b.shape)],
        out_specs=_rows(TM_FFN, D_MODEL),
        out_shape=jax.ShapeDtypeStruct((ROWS, D_MODEL), F32),
        compiler_params=pltpu.CompilerParams(
            dimension_semantics=("arbitrary",), vmem_limit_bytes=VMEM_LIMIT_BYTES),
        name="ffn_ln",
    )(x, wg, wu, wo, g, b)


def _ffn_ple(x, p, wg, wu, wo, g3, b3, wpi, wpg, g4, b4, alpha):
    return pl.pallas_call(
        functools.partial(_ffn_ple_kernel, alpha),
        grid=(ROWS // TM_FFN,),
        in_specs=[_rows(TM_FFN, D_MODEL), _rows(TM_FFN, PLE_DIM), _resident(wg.shape),
                  _resident(wu.shape), _resident(wo.shape), _resident(g3.shape),
                  _resident(b3.shape), _resident(wpi.shape), _resident(wpg.shape),
                  _resident(g4.shape), _resident(b4.shape)],
        out_specs=_rows(TM_FFN, D_MODEL),
        out_shape=jax.ShapeDtypeStruct((ROWS, D_MODEL), F32),
        compiler_params=pltpu.CompilerParams(
            dimension_semantics=("arbitrary",), vmem_limit_bytes=VMEM_LIMIT_BYTES),
        name="ffn_ple",
    )(x, p, wg, wu, wo, g3, b3, wpi, wpg, g4, b4)


def _mixer(x, win, cw, cb, cwo, lam, bd, cd, dsk, wglu, wout, g, b, alpha):
    consts = (win, cw, cb, cwo, lam, bd, cd, dsk, wglu, wout, g, b)
    return pl.pallas_call(
        functools.partial(_mixer_kernel, alpha),
        grid=(ROWS // TM_MIX,),
        in_specs=[_rows(TM_MIX, D_MODEL)] + [_resident(c.shape) for c in consts],
        out_specs=_rows(TM_MIX, D_MODEL),
        out_shape=jax.ShapeDtypeStruct((ROWS, D_MODEL), F32),
        scratch_shapes=[
            pltpu.VMEM((TM_MIX, 4 * HALF_ST), F32),
            pltpu.VMEM((SUBLANES, 4 * HALF_ST), F32),
            pltpu.VMEM((CONV_HALO + TM_MIX, CONV_CH), F32),
        ],
        compiler_params=pltpu.CompilerParams(
            dimension_semantics=("arbitrary",), vmem_limit_bytes=VMEM_LIMIT_BYTES),
        name="mixer",
    )(x, *consts)


def _block_diag(blocks):
    n, a, b = blocks.shape
    eye = jnp.eye(n, dtype=blocks.dtype)
    return jnp.einsum("gab,gk->gakb", blocks, eye).reshape(n * a, n * b)


def _ssm_matrices(lam_re, lam_im, log_step, b_re, b_im, c_re, c_im):
    dt = jnp.exp(log_step)[:, None]
    mag = jnp.exp(lam_re * dt)
    lbar_re = mag * jnp.cos(lam_im * dt)
    lbar_im = mag * jnp.sin(lam_im * dt)
    den = lam_re * lam_re + lam_im * lam_im
    q_re = ((lbar_re - 1.0) * lam_re + lbar_im * lam_im) / den
    q_im = (lbar_im * lam_re - (lbar_re - 1.0) * lam_im) / den
    bbar_re = q_re[..., None] * b_re - q_im[..., None] * b_im
    bbar_im = q_re[..., None] * b_im + q_im[..., None] * b_re

    lam_cols, bd, cd = [], [], []
    for h in range(2):
        gs = slice(h * HALF_GROUPS, (h + 1) * HALF_GROUPS)
        lam_cols += [lbar_re[gs].reshape(-1), lbar_im[gs].reshape(-1)]
        bd.append(jnp.concatenate(
            [_block_diag(jnp.swapaxes(bbar_re[gs], 1, 2)),
             _block_diag(jnp.swapaxes(bbar_im[gs], 1, 2))], axis=1))
        cd.append(jnp.concatenate(
            [_block_diag(jnp.swapaxes(c_re[gs], 1, 2)),
             _block_diag(jnp.swapaxes(-c_im[gs], 1, 2))], axis=0))
    lam = jnp.broadcast_to(jnp.concatenate(lam_cols)[None, :], (SUBLANES, 4 * HALF_ST))
    return lam, jnp.stack(bd).astype(BF16), jnp.stack(cd).astype(BF16)


def kernel(x, p, ffn1_w_in, ffn1_w_out, ln1_g, ln1_b, mix_w_in, conv_w, conv_b, conv_w_out, ssm_lam_re, ssm_lam_im, ssm_log_step, ssm_b_re, ssm_b_im, ssm_c_re, ssm_c_im, ssm_d, ssm_w_glu, mix_w_out, ln2_g, ln2_b, ffn2_w_in, ffn2_w_out, ln3_g, ln3_b, ple_w_in, ple_w_gate, ln4_g, ln4_b):
    depth = p.shape[0]
    alpha = (2.0 * depth) ** 0.25
    assert x.shape == (BATCH, SEQ, D_MODEL) and BATCH == SUBLANES

    def row(v):
        return v.reshape(1, -1)

    h = jnp.swapaxes(x, 0, 1).reshape(ROWS, D_MODEL)
    for i in range(depth):
        pi = jnp.swapaxes(p[i], 0, 1).reshape(ROWS, PLE_DIM)
        w1 = ffn1_w_in[i].astype(BF16)
        h = _ffn_ln(h, w1[:, :D_FF], w1[:, D_FF:], ffn1_w_out[i].astype(BF16),
                    row(ln1_g[i]), row(ln1_b[i]), alpha)
        lam, bd, cd = _ssm_matrices(ssm_lam_re[i], ssm_lam_im[i], ssm_log_step[i],
                                    ssm_b_re[i], ssm_b_im[i], ssm_c_re[i], ssm_c_im[i])
        h = _mixer(h, mix_w_in[i].astype(BF16), conv_w[i], row(conv_b[i]),
                   conv_w_out[i].astype(BF16), lam, bd, cd, row(ssm_d[i]),
                   ssm_w_glu[i].astype(BF16), mix_w_out[i].astype(BF16),
                   row(ln2_g[i]), row(ln2_b[i]), alpha)
        w2 = ffn2_w_in[i].astype(BF16)
        h = _ffn_ple(h, pi, w2[:, :D_FF], w2[:, D_FF:], ffn2_w_out[i].astype(BF16),
                     row(ln3_g[i]), row(ln3_b[i]), ple_w_in[i].astype(BF16),
                     ple_w_gate[i].astype(BF16), row(ln4_g[i]), row(ln4_b[i]), alpha)
    return jnp.swapaxes(h.reshape(SEQ, BATCH, D_MODEL), 0, 1)
```

```python
import functools

import jax
import jax.numpy as jnp
from jax import lax
from jax.experimental import pallas as pl
from jax.experimental.pallas import tpu as pltpu

D_MODEL = 1024
BATCH = 8
SEQ = 2048
PLE_DIM = 256
D_FF = 2816
CONV_CH = 512
CONV_K = 3
SSM_WIDTH = 512
SSM_GROUP = 16
SSM_GROUPS = SSM_WIDTH // SSM_GROUP
SSM_STATE = 64
LN_EPS = 1e-5

SUBLANES = 8
VMEM_LIMIT_BYTES = 56 * 1024 * 1024

ROWS = BATCH * SEQ
TM_FFN = 512
TL_MIX = 64
TM_MIX = BATCH * TL_MIX
FF_CHUNKS = ((0, 768), (768, 768), (1536, 768), (2304, 512))
HALF_GROUPS = SSM_GROUPS // 2
HALF_IN = HALF_GROUPS * SSM_GROUP
HALF_ST = HALF_GROUPS * SSM_STATE
CONV_PAD = SUBLANES

BF16 = jnp.bfloat16
F32 = jnp.float32


def _dot(a, b):
    return jnp.dot(a, b, preferred_element_type=F32)


def _layer_norm(y, g, b):
    mu = jnp.mean(y, axis=-1, keepdims=True)
    yc = y - mu
    var = jnp.mean(yc * yc, axis=-1, keepdims=True)
    return yc * lax.rsqrt(var + LN_EPS) * g + b


def _swiglu(xb, wg_ref, wu_ref, wo_ref):
    acc = None
    for c0, cw in FF_CHUNKS:
        gate = _dot(xb, wg_ref[:, c0:c0 + cw])
        up = _dot(xb, wu_ref[:, c0:c0 + cw])
        h = (gate * jax.nn.sigmoid(gate) * up).astype(BF16)
        part = _dot(h, wo_ref[c0:c0 + cw, :])
        acc = part if acc is None else acc + part
    return acc


def _ffn_ln_kernel(alpha, x_ref, wg_ref, wu_ref, wo_ref, g_ref, b_ref, o_ref):
    x = x_ref[...]
    f = _swiglu(x.astype(BF16), wg_ref, wu_ref, wo_ref)
    o_ref[...] = _layer_norm(alpha * x + 0.5 * f, g_ref[...], b_ref[...])


def _ffn_ple_kernel(alpha, x_ref, p_ref, wg_ref, wu_ref, wo_ref, g3_ref, b3_ref,
                    wpi_ref, wpg_ref, g4_ref, b4_ref, o_ref):
    x = x_ref[...]
    f = _swiglu(x.astype(BF16), wg_ref, wu_ref, wo_ref)
    x3 = _layer_norm(alpha * x + 0.5 * f, g3_ref[...], b3_ref[...])
    e = _dot(p_ref[...].astype(BF16), wpi_ref[...]) * jax.nn.sigmoid(
        _dot(x3.astype(BF16), wpg_ref[...]))
    o_ref[...] = _layer_norm(alpha * x3 + e, g4_ref[...], b4_ref[...])


def _mixer_kernel(alpha, x_ref, win_ref, cw_ref, cb_ref, cwo_ref, lam_ref, bd_ref,
                  cd_ref, dsk_ref, perm_ref, unperm_ref, wglu_ref, wout_ref, g_ref,
                  b_ref, o_ref, s_ref, st_ref, v_ref):
    tl = x_ref.shape[1]
    tm = BATCH * tl
    step = pl.program_id(0)

    @pl.when(step == 0)
    def _():
        st_ref[...] = jnp.zeros_like(st_ref)
        v_ref[:, 0:CONV_PAD, :] = jnp.zeros((BATCH, CONV_PAD, CONV_CH), F32)

    @pl.when(step > 0)
    def _():
        v_ref[:, 0:CONV_PAD, :] = v_ref[:, tl:tl + CONV_PAD, :]

    x = x_ref[...].reshape(tm, D_MODEL)
    xb = x.astype(BF16)

    def proj(c0, cw):
        return _dot(xb, win_ref[:, c0:c0 + cw])

    v = proj(CONV_CH, CONV_CH) * proj(2 * CONV_CH, CONV_CH)
    v_ref[:, CONV_PAD:CONV_PAD + tl, :] = v.reshape(BATCH, tl, CONV_CH)
    z = cb_ref[...].reshape(1, 1, CONV_CH)
    for k in range(CONV_K):
        lag = CONV_K - 1 - k
        z = z + cw_ref[k:k + 1, :].reshape(1, 1, CONV_CH) * v_ref[
            :, CONV_PAD - lag:CONV_PAD - lag + tl, :]
    y_conv = _dot((proj(0, CONV_CH) * z.reshape(tm, CONV_CH)).astype(BF16), cwo_ref[...])
    merged = jax.nn.sigmoid(proj(4 * CONV_CH, D_MODEL)) * y_conv

    u = proj(3 * CONV_CH, SSM_WIDTH)
    ub = _dot(perm_ref[...], u.astype(BF16)).astype(BF16)
    for h in range(2):
        s_ref[:, 2 * h * HALF_ST:2 * (h + 1) * HALF_ST] = _dot(
            ub[:, h * HALF_IN:(h + 1) * HALF_IN], bd_ref[h])

    for h in range(2):
        re0 = 2 * h * HALF_ST
        im0 = re0 + HALF_ST
        lam_re = lam_ref[:, re0:re0 + HALF_ST]
        lam_im = lam_ref[:, im0:im0 + HALF_ST]

        def body(l, carry, re0=re0, im0=im0, lam_re=lam_re, lam_im=lam_im):
            sr, si = carry
            r0 = pl.multiple_of(l * SUBLANES, SUBLANES)
            nr = lam_re * sr - lam_im * si + s_ref[pl.ds(r0, SUBLANES), re0:re0 + HALF_ST]
            ni = lam_re * si + lam_im * sr + s_ref[pl.ds(r0, SUBLANES), im0:im0 + HALF_ST]
            s_ref[pl.ds(r0, SUBLANES), re0:re0 + HALF_ST] = nr
            s_ref[pl.ds(r0, SUBLANES), im0:im0 + HALF_ST] = ni
            return nr, ni

        sr, si = lax.fori_loop(
            0, tl, body,
            (st_ref[:, re0:re0 + HALF_ST], st_ref[:, im0:im0 + HALF_ST]), unroll=4)
        st_ref[:, re0:re0 + HALF_ST] = sr
        st_ref[:, im0:im0 + HALF_ST] = si

    ys = []
    for h in range(2):
        sb = s_ref[:, 2 * h * HALF_ST:2 * (h + 1) * HALF_ST].astype(BF16)
        ys.append(_dot(sb, cd_ref[h]))
    y_tb = jnp.concatenate(ys, axis=-1)
    hi = y_tb.astype(BF16)
    lo = (y_tb - hi.astype(F32)).astype(BF16)
    y = _dot(unperm_ref[...], jnp.concatenate([hi, lo], axis=0)) + dsk_ref[...] * u
    sg = jax.nn.gelu(y).astype(BF16)
    y_ssm = _dot(sg, wglu_ref[:, 0:D_MODEL]) * jax.nn.sigmoid(
        _dot(sg, wglu_ref[:, D_MODEL:2 * D_MODEL]))
    merged = merged + jax.nn.sigmoid(proj(4 * CONV_CH + D_MODEL, D_MODEL)) * y_ssm

    mix = _dot(merged.astype(BF16), wout_ref[...])
    o_ref[...] = _layer_norm(alpha * x + mix, g_ref[...], b_ref[...]).reshape(
        BATCH, tl, D_MODEL)


def _resident(shape):
    zeros = (0,) * len(shape)
    return pl.BlockSpec(shape, lambda i: zeros, pipeline_mode=pl.Buffered(1))


def _rows(tm, width):
    return pl.BlockSpec((tm, width), lambda i: (i, 0))


def _ffn_ln(x, wg, wu, wo, g, b, alpha):
    return pl.pallas_call(
        functools.partial(_ffn_ln_kernel, alpha),
        grid=(ROWS // TM_FFN,),
        in_specs=[_rows(TM_FFN, D_MODEL), _resident(wg.shape), _resident(wu.shape),
                  _resident(wo.shape), _resident(g.shape), _resident(b.shape)],
        out_specs=_rows(TM_FFN, D_MODEL),
        out_shape=jax.ShapeDtypeStruct((ROWS, D_MODEL), F32),
        compiler_params=pltpu.CompilerParams(
            dimension_semantics=("arbitrary",), vmem_limit_bytes=VMEM_LIMIT_BYTES),
        name="ffn_ln",
    )(x, wg, wu, wo, g, b)


def _ffn_ple(x, p, wg, wu, wo, g3, b3, wpi, wpg, g4, b4, alpha):
    return pl.pallas_call(
        functools.partial(_ffn_ple_kernel, alpha),
        grid=(ROWS // TM_FFN,),
        in_specs=[_rows(TM_FFN, D_MODEL), _rows(TM_FFN, PLE_DIM), _resident(wg.shape),
                  _resident(wu.shape), _resident(wo.shape), _resident(g3.shape),
                  _resident(b3.shape), _resident(wpi.shape), _resident(wpg.shape),
                  _resident(g4.shape), _resident(b4.shape)],
        out_specs=_rows(TM_FFN, D_MODEL),
        out_shape=jax.ShapeDtypeStruct((ROWS, D_MODEL), F32),
        compiler_params=pltpu.CompilerParams(
            dimension_semantics=("arbitrary",), vmem_limit_bytes=VMEM_LIMIT_BYTES),
        name="ffn_ple",
    )(x, p, wg, wu, wo, g3, b3, wpi, wpg, g4, b4)


def _mixer(x, win, cw, cb, cwo, lam, bd, cd, dsk, perm, unperm, wglu, wout, g, b, alpha):
    consts = (win, cw, cb, cwo, lam, bd, cd, dsk, perm, unperm, wglu, wout, g, b)
    seq_block = pl.BlockSpec((BATCH, TL_MIX, D_MODEL), lambda i: (0, i, 0))
    return pl.pallas_call(
        functools.partial(_mixer_kernel, alpha),
        grid=(SEQ // TL_MIX,),
        in_specs=[seq_block] + [_resident(c.shape) for c in consts],
        out_specs=seq_block,
        out_shape=jax.ShapeDtypeStruct((BATCH, SEQ, D_MODEL), F32),
        scratch_shapes=[
            pltpu.VMEM((TM_MIX, 4 * HALF_ST), F32),
            pltpu.VMEM((SUBLANES, 4 * HALF_ST), F32),
            pltpu.VMEM((BATCH, CONV_PAD + TL_MIX, CONV_CH), F32),
        ],
        compiler_params=pltpu.CompilerParams(
            dimension_semantics=("arbitrary",), vmem_limit_bytes=VMEM_LIMIT_BYTES),
        name="mixer",
    )(x, *consts)


def _block_diag(blocks):
    n, a, b = blocks.shape
    eye = jnp.eye(n, dtype=blocks.dtype)
    return jnp.einsum("gab,gk->gakb", blocks, eye).reshape(n * a, n * b)


def _ssm_matrices(lam_re, lam_im, log_step, b_re, b_im, c_re, c_im):
    dt = jnp.exp(log_step)[:, None]
    mag = jnp.exp(lam_re * dt)
    lbar_re = mag * jnp.cos(lam_im * dt)
    lbar_im = mag * jnp.sin(lam_im * dt)
    den = lam_re * lam_re + lam_im * lam_im
    q_re = ((lbar_re - 1.0) * lam_re + lbar_im * lam_im) / den
    q_im = (lbar_im * lam_re - (lbar_re - 1.0) * lam_im) / den
    bbar_re = q_re[..., None] * b_re - q_im[..., None] * b_im
    bbar_im = q_re[..., None] * b_im + q_im[..., None] * b_re

    lam_cols, bd, cd = [], [], []
    for h in range(2):
        gs = slice(h * HALF_GROUPS, (h + 1) * HALF_GROUPS)
        lam_cols += [lbar_re[gs].reshape(-1), lbar_im[gs].reshape(-1)]
        bd.append(jnp.concatenate(
            [_block_diag(jnp.swapaxes(bbar_re[gs], 1, 2)),
             _block_diag(jnp.swapaxes(bbar_im[gs], 1, 2))], axis=1))
        cd.append(jnp.concatenate(
            [_block_diag(jnp.swapaxes(c_re[gs], 1, 2)),
             _block_diag(jnp.swapaxes(-c_im[gs], 1, 2))], axis=0))
    lam = jnp.broadcast_to(jnp.concatenate(lam_cols)[None, :], (SUBLANES, 4 * HALF_ST))
    return lam, jnp.stack(bd).astype(BF16), jnp.stack(cd).astype(BF16)


def _row_permutations():
    dst = jnp.arange(TM_MIX)
    src = (dst % BATCH) * TL_MIX + dst // BATCH
    perm = (src[:, None] == jnp.arange(TM_MIX)[None, :]).astype(BF16)
    return perm, jnp.concatenate([perm.T, perm.T], axis=1)


def kernel(x, p, ffn1_w_in, ffn1_w_out, ln1_g, ln1_b, mix_w_in, conv_w, conv_b, conv_w_out, ssm_lam_re, ssm_lam_im, ssm_log_step, ssm_b_re, ssm_b_im, ssm_c_re, ssm_c_im, ssm_d, ssm_w_glu, mix_w_out, ln2_g, ln2_b, ffn2_w_in, ffn2_w_out, ln3_g, ln3_b, ple_w_in, ple_w_gate, ln4_g, ln4_b):
    depth = p.shape[0]
    alpha = (2.0 * depth) ** 0.25
    assert x.shape == (BATCH, SEQ, D_MODEL) and BATCH == SUBLANES

    def row(v):
        return v.reshape(1, -1)

    perm, unperm = _row_permutations()
    h = x.reshape(ROWS, D_MODEL)
    for i in range(depth):
        w1 = ffn1_w_in[i].astype(BF16)
        h = _ffn_ln(h, w1[:, :D_FF], w1[:, D_FF:], ffn1_w_out[i].astype(BF16),
                    row(ln1_g[i]), row(ln1_b[i]), alpha)
        lam, bd, cd = _ssm_matrices(ssm_lam_re[i], ssm_lam_im[i], ssm_log_step[i],
                                    ssm_b_re[i], ssm_b_im[i], ssm_c_re[i], ssm_c_im[i])
        h = _mixer(h.reshape(BATCH, SEQ, D_MODEL), mix_w_in[i].astype(BF16), conv_w[i],
                   row(conv_b[i]), conv_w_out[i].astype(BF16), lam, bd, cd, row(ssm_d[i]),
                   perm, unperm, ssm_w_glu[i].astype(BF16), mix_w_out[i].astype(BF16),
                   row(ln2_g[i]), row(ln2_b[i]), alpha).reshape(ROWS, D_MODEL)
        w2 = ffn2_w_in[i].astype(BF16)
        h = _ffn_ple(h, p[i].reshape(ROWS, PLE_DIM), w2[:, :D_FF], w2[:, D_FF:],
                     ffn2_w_out[i].astype(BF16), row(ln3_g[i]), row(ln3_b[i]),
                     ple_w_in[i].astype(BF16), ple_w_gate[i].astype(BF16),
                     row(ln4_g[i]), row(ln4_b[i]), alpha)
    return h.reshape(BATCH, SEQ, D_MODEL)
```

```python
import functools

import jax
import jax.numpy as jnp
from jax import lax
from jax.experimental import pallas as pl
from jax.experimental.pallas import tpu as pltpu

D_MODEL = 1024
BATCH = 8
SEQ = 2048
PLE_DIM = 256
D_FF = 2816
CONV_CH = 512
CONV_K = 3
SSM_WIDTH = 512
SSM_GROUP = 16
SSM_GROUPS = SSM_WIDTH // SSM_GROUP
SSM_STATE = 64
LN_EPS = 1e-5

SUBLANES = 8
BF16_ROWS = 16
VMEM_LIMIT_BYTES = 56 * 1024 * 1024

ROWS = BATCH * SEQ
TM_FFN = 512
TL_MIX = 64
TM_MIX = BATCH * TL_MIX
FF_CHUNKS = ((0, 768), (768, 768), (1536, 768), (2304, 512))
HALF_GROUPS = SSM_GROUPS // 2
HALF_IN = HALF_GROUPS * SSM_GROUP
HALF_ST = HALF_GROUPS * SSM_STATE
CONV_PAD = SUBLANES

BF16 = jnp.bfloat16
F32 = jnp.float32


def _dot(a, b):
    return jnp.dot(a, b, preferred_element_type=F32)


def _layer_norm(y, g, b):
    mu = jnp.mean(y, axis=-1, keepdims=True)
    yc = y - mu
    var = jnp.mean(yc * yc, axis=-1, keepdims=True)
    return yc * lax.rsqrt(var + LN_EPS) * g + b


def _swiglu(xb, win_ref, wo_ref):
    acc = None
    for c0, cw in FF_CHUNKS:
        gate = _dot(xb, win_ref[:, c0:c0 + cw])
        up = _dot(xb, win_ref[:, D_FF + c0:D_FF + c0 + cw])
        h = (gate * jax.nn.sigmoid(gate) * up).astype(BF16)
        part = _dot(h, wo_ref[c0:c0 + cw, :])
        acc = part if acc is None else acc + part
    return acc


def _cast_blocks(src_refs, dst_refs):
    for src, dst in zip(src_refs, dst_refs, strict=True):
        dst[...] = src[...].astype(BF16)


def _ffn_ln_kernel(alpha, n_cast, x_ref, win_ref, wo_ref, g_ref, b_ref, *rest):
    cast_src, (o_ref, *cast_dst) = rest[:n_cast], rest[n_cast:]
    _cast_blocks(cast_src, cast_dst)
    x = x_ref[...]
    f = _swiglu(x.astype(BF16), win_ref, wo_ref)
    o_ref[...] = _layer_norm(alpha * x + 0.5 * f, g_ref[...], b_ref[...])


def _ffn_ple_kernel(alpha, x_ref, p_ref, win_ref, wo_ref, g3_ref, b3_ref,
                    wpi_ref, wpg_ref, g4_ref, b4_ref, o_ref):
    x = x_ref[...]
    f = _swiglu(x.astype(BF16), win_ref, wo_ref)
    x3 = _layer_norm(alpha * x + 0.5 * f, g3_ref[...], b3_ref[...])
    e = _dot(p_ref[...].astype(BF16), wpi_ref[...]) * jax.nn.sigmoid(
        _dot(x3.astype(BF16), wpg_ref[...]))
    o_ref[...] = _layer_norm(alpha * x3 + e, g4_ref[...], b4_ref[...])


def _mixer_kernel(alpha, n_cast, x_ref, win_ref, cw_ref, cb_ref, cwo_ref, lam_ref, bd_ref,
                  cd_ref, dsk_ref, perm_ref, unperm_ref, wglu_ref, wout_ref, g_ref,
                  b_ref, *rest):
    cast_src, o_ref = rest[:n_cast], rest[n_cast]
    cast_dst, (s_ref, st_ref, v_ref) = rest[n_cast + 1:2 * n_cast + 1], rest[2 * n_cast + 1:]
    _cast_blocks(cast_src, cast_dst)
    tl = x_ref.shape[1]
    tm = BATCH * tl
    step = pl.program_id(0)

    @pl.when(step == 0)
    def _():
        st_ref[...] = jnp.zeros_like(st_ref)
        v_ref[:, 0:CONV_PAD, :] = jnp.zeros((BATCH, CONV_PAD, CONV_CH), F32)

    @pl.when(step > 0)
    def _():
        v_ref[:, 0:CONV_PAD, :] = v_ref[:, tl:tl + CONV_PAD, :]

    x = x_ref[...].reshape(tm, D_MODEL)
    xb = x.astype(BF16)

    def proj(c0, cw):
        return _dot(xb, win_ref[:, c0:c0 + cw])

    v = proj(CONV_CH, CONV_CH) * proj(2 * CONV_CH, CONV_CH)
    v_ref[:, CONV_PAD:CONV_PAD + tl, :] = v.reshape(BATCH, tl, CONV_CH)
    z = cb_ref[...].reshape(1, 1, CONV_CH)
    for k in range(CONV_K):
        lag = CONV_K - 1 - k
        z = z + cw_ref[k:k + 1, :].reshape(1, 1, CONV_CH) * v_ref[
            :, CONV_PAD - lag:CONV_PAD - lag + tl, :]
    y_conv = _dot((proj(0, CONV_CH) * z.reshape(tm, CONV_CH)).astype(BF16), cwo_ref[...])
    merged = jax.nn.sigmoid(proj(4 * CONV_CH, D_MODEL)) * y_conv

    u = proj(3 * CONV_CH, SSM_WIDTH)
    ub = _dot(perm_ref[...], u.astype(BF16)).astype(BF16)
    for h in range(2):
        s_ref[:, 2 * h * HALF_ST:2 * (h + 1) * HALF_ST] = _dot(
            ub[:, h * HALF_IN:(h + 1) * HALF_IN], bd_ref[h])

    for h in range(2):
        re0 = 2 * h * HALF_ST
        im0 = re0 + HALF_ST
        lam_re = lam_ref[:, re0:re0 + HALF_ST]
        lam_im = lam_ref[:, im0:im0 + HALF_ST]

        def body(l, carry, re0=re0, im0=im0, lam_re=lam_re, lam_im=lam_im):
            sr, si = carry
            r0 = pl.multiple_of(l * SUBLANES, SUBLANES)
            nr = lam_re * sr - lam_im * si + s_ref[pl.ds(r0, SUBLANES), re0:re0 + HALF_ST]
            ni = lam_re * si + lam_im * sr + s_ref[pl.ds(r0, SUBLANES), im0:im0 + HALF_ST]
            s_ref[pl.ds(r0, SUBLANES), re0:re0 + HALF_ST] = nr
            s_ref[pl.ds(r0, SUBLANES), im0:im0 + HALF_ST] = ni
            return nr, ni

        sr, si = lax.fori_loop(
            0, tl, body,
            (st_ref[:, re0:re0 + HALF_ST], st_ref[:, im0:im0 + HALF_ST]), unroll=4)
        st_ref[:, re0:re0 + HALF_ST] = sr
        st_ref[:, im0:im0 + HALF_ST] = si

    ys = []
    for h in range(2):
        sb = s_ref[:, 2 * h * HALF_ST:2 * (h + 1) * HALF_ST].astype(BF16)
        ys.append(_dot(sb, cd_ref[h]))
    y_tb = jnp.concatenate(ys, axis=-1)
    hi = y_tb.astype(BF16)
    lo = (y_tb - hi.astype(F32)).astype(BF16)
    y = _dot(unperm_ref[...], jnp.concatenate([hi, lo], axis=0)) + dsk_ref[...] * u
    sg = jax.nn.gelu(y).astype(BF16)
    y_ssm = _dot(sg, wglu_ref[:, 0:D_MODEL]) * jax.nn.sigmoid(
        _dot(sg, wglu_ref[:, D_MODEL:2 * D_MODEL]))
    merged = merged + jax.nn.sigmoid(proj(4 * CONV_CH + D_MODEL, D_MODEL)) * y_ssm

    mix = _dot(merged.astype(BF16), wout_ref[...])
    o_ref[...] = _layer_norm(alpha * x + mix, g_ref[...], b_ref[...]).reshape(
        BATCH, tl, D_MODEL)


def _resident(shape):
    zeros = (0,) * len(shape)
    return pl.BlockSpec(shape, lambda i: zeros, pipeline_mode=pl.Buffered(1))


def _rows(tm, width):
    return pl.BlockSpec((tm, width), lambda i: (i, 0))


def _cast_plan(weights, n_steps):
    specs, shapes = [], []
    for w in weights:
        rows, cols = w.shape
        n_blocks = n_steps
        while (rows // n_blocks) % BF16_ROWS or rows % n_blocks:
            n_blocks //= 2
        repeat = n_steps // n_blocks
        specs.append(pl.BlockSpec((rows // n_blocks, cols),
                                  lambda i, repeat=repeat: (i // repeat, 0)))
        shapes.append(jax.ShapeDtypeStruct(w.shape, BF16))
    return specs, shapes


def _ffn_ln(x, win, wo, g, b, next_weights, alpha):
    n_steps = ROWS // TM_FFN
    cast_specs, cast_shapes = _cast_plan(next_weights, n_steps)
    return pl.pallas_call(
        functools.partial(_ffn_ln_kernel, alpha, len(next_weights)),
        grid=(n_steps,),
        in_specs=[_rows(TM_FFN, D_MODEL), _resident(win.shape), _resident(wo.shape),
                  _resident(g.shape), _resident(b.shape)] + cast_specs,
        out_specs=[_rows(TM_FFN, D_MODEL)] + cast_specs,
        out_shape=[jax.ShapeDtypeStruct((ROWS, D_MODEL), F32)] + cast_shapes,
        compiler_params=pltpu.CompilerParams(
            dimension_semantics=("arbitrary",), vmem_limit_bytes=VMEM_LIMIT_BYTES),
        name="ffn_ln",
    )(x, win, wo, g, b, *next_weights)


def _ffn_ple(x, p, win, wo, g3, b3, wpi, wpg, g4, b4, alpha):
    return pl.pallas_call(
        functools.partial(_ffn_ple_kernel, alpha),
        grid=(ROWS // TM_FFN,),
        in_specs=[_rows(TM_FFN, D_MODEL), _rows(TM_FFN, PLE_DIM), _resident(win.shape),
                  _resident(wo.shape), _resident(g3.shape),
                  _resident(b3.shape), _resident(wpi.shape), _resident(wpg.shape),
                  _resident(g4.shape), _resident(b4.shape)],
        out_specs=_rows(TM_FFN, D_MODEL),
        out_shape=jax.ShapeDtypeStruct((ROWS, D_MODEL), F32),
        compiler_params=pltpu.CompilerParams(
            dimension_semantics=("arbitrary",), vmem_limit_bytes=VMEM_LIMIT_BYTES),
        name="ffn_ple",
    )(x, p, win, wo, g3, b3, wpi, wpg, g4, b4)


def _mixer(x, win, cw, cb, cwo, lam, bd, cd, dsk, perm, unperm, wglu, wout, g, b,
           next_weights, alpha):
    consts = (win, cw, cb, cwo, lam, bd, cd, dsk, perm, unperm, wglu, wout, g, b)
    n_steps = SEQ // TL_MIX
    cast_specs, cast_shapes = _cast_plan(next_weights, n_steps)
    seq_block = pl.BlockSpec((BATCH, TL_MIX, D_MODEL), lambda i: (0, i, 0))
    return pl.pallas_call(
        functools.partial(_mixer_kernel, alpha, len(next_weights)),
        grid=(n_steps,),
        in_specs=[seq_block] + [_resident(c.shape) for c in consts] + cast_specs,
        out_specs=[seq_block] + cast_specs,
        out_shape=[jax.ShapeDtypeStruct((BATCH, SEQ, D_MODEL), F32)] + cast_shapes,
        scratch_shapes=[
            pltpu.VMEM((TM_MIX, 4 * HALF_ST), F32),
            pltpu.VMEM((SUBLANES, 4 * HALF_ST), F32),
            pltpu.VMEM((BATCH, CONV_PAD + TL_MIX, CONV_CH), F32),
        ],
        compiler_params=pltpu.CompilerParams(
            dimension_semantics=("arbitrary",), vmem_limit_bytes=VMEM_LIMIT_BYTES),
        name="mixer",
    )(x, *consts, *next_weights)


def _block_diag(blocks):
    n, a, b = blocks.shape
    eye = jnp.eye(n, dtype=blocks.dtype)
    return jnp.einsum("gab,gk->gakb", blocks, eye).reshape(n * a, n * b)


def _ssm_matrices(lam_re, lam_im, log_step, b_re, b_im, c_re, c_im):
    dt = jnp.exp(log_step)[:, None]
    mag = jnp.exp(lam_re * dt)
    lbar_re = mag * jnp.cos(lam_im * dt)
    lbar_im = mag * jnp.sin(lam_im * dt)
    den = lam_re * lam_re + lam_im * lam_im
    q_re = ((lbar_re - 1.0) * lam_re + lbar_im * lam_im) / den
    q_im = (lbar_im * lam_re - (lbar_re - 1.0) * lam_im) / den
    bbar_re = q_re[..., None] * b_re - q_im[..., None] * b_im
    bbar_im = q_re[..., None] * b_im + q_im[..., None] * b_re

    lam_cols, bd, cd = [], [], []
    for h in range(2):
        gs = slice(h * HALF_GROUPS, (h + 1) * HALF_GROUPS)
        lam_cols += [lbar_re[gs].reshape(-1), lbar_im[gs].reshape(-1)]
        bd.append(jnp.concatenate(
            [_block_diag(jnp.swapaxes(bbar_re[gs], 1, 2)),
             _block_diag(jnp.swapaxes(bbar_im[gs], 1, 2))], axis=1))
        cd.append(jnp.concatenate(
            [_block_diag(jnp.swapaxes(c_re[gs], 1, 2)),
             _block_diag(jnp.swapaxes(-c_im[gs], 1, 2))], axis=0))
    lam = jnp.broadcast_to(jnp.concatenate(lam_cols)[None, :], (SUBLANES, 4 * HALF_ST))
    return lam, jnp.stack(bd).astype(BF16), jnp.stack(cd).astype(BF16)


def _row_permutations():
    dst = jnp.arange(TM_MIX)
    src = (dst % BATCH) * TL_MIX + dst // BATCH
    perm = (src[:, None] == jnp.arange(TM_MIX)[None, :]).astype(BF16)
    return perm, jnp.concatenate([perm.T, perm.T], axis=1)


def kernel(x, p, ffn1_w_in, ffn1_w_out, ln1_g, ln1_b, mix_w_in, conv_w, conv_b, conv_w_out, ssm_lam_re, ssm_lam_im, ssm_log_step, ssm_b_re, ssm_b_im, ssm_c_re, ssm_c_im, ssm_d, ssm_w_glu, mix_w_out, ln2_g, ln2_b, ffn2_w_in, ffn2_w_out, ln3_g, ln3_b, ple_w_in, ple_w_gate, ln4_g, ln4_b):
    depth = p.shape[0]
    alpha = (2.0 * depth) ** 0.25
    assert x.shape == (BATCH, SEQ, D_MODEL) and BATCH == SUBLANES

    def row(v):
        return v.reshape(1, -1)

    perm, unperm = _row_permutations()
    h = x.reshape(ROWS, D_MODEL)
    for i in range(depth):
        h, mw_in, cw_out, w_glu, mw_out = _ffn_ln(
            h, ffn1_w_in[i].astype(BF16), ffn1_w_out[i].astype(BF16), row(ln1_g[i]),
            row(ln1_b[i]), (mix_w_in[i], conv_w_out[i], ssm_w_glu[i], mix_w_out[i]), alpha)
        lam, bd, cd = _ssm_matrices(ssm_lam_re[i], ssm_lam_im[i], ssm_log_step[i],
                                    ssm_b_re[i], ssm_b_im[i], ssm_c_re[i], ssm_c_im[i])
        h, f2_in, f2_out, pw_in, pw_gate = _mixer(
            h.reshape(BATCH, SEQ, D_MODEL), mw_in, conv_w[i], row(conv_b[i]), cw_out, lam,
            bd, cd, row(ssm_d[i]), perm, unperm, w_glu, mw_out, row(ln2_g[i]), row(ln2_b[i]),
            (ffn2_w_in[i], ffn2_w_out[i], ple_w_in[i], ple_w_gate[i]), alpha)
        h = _ffn_ple(h.reshape(ROWS, D_MODEL), p[i].reshape(ROWS, PLE_DIM), f2_in, f2_out,
                     row(ln3_g[i]), row(ln3_b[i]), pw_in, pw_gate,
                     row(ln4_g[i]), row(ln4_b[i]), alpha)
    return h.reshape(BATCH, SEQ, D_MODEL)
```

```python
import functools

import jax
import jax.numpy as jnp
from jax import lax
from jax.experimental import pallas as pl
from jax.experimental.pallas import tpu as pltpu

D_MODEL = 1024
BATCH = 8
SEQ = 2048
PLE_DIM = 256
D_FF = 2816
CONV_CH = 512
CONV_K = 3
SSM_WIDTH = 512
SSM_GROUP = 16
SSM_GROUPS = SSM_WIDTH // SSM_GROUP
SSM_STATE = 64
LN_EPS = 1e-5

SUBLANES = 8
LANES = 128
BF16_ROWS = 16
VMEM_LIMIT_BYTES = 56 * 1024 * 1024

ROWS = BATCH * SEQ
TM_FFN = 512
TL_MIX = 64
TM_MIX = BATCH * TL_MIX
FF_CHUNKS = ((0, 768), (768, 768), (1536, 768), (2304, 512))
HALF_GROUPS = SSM_GROUPS // 2
HALF_IN = HALF_GROUPS * SSM_GROUP
HALF_ST = HALF_GROUPS * SSM_STATE
CONV_PAD = SUBLANES

BF16 = jnp.bfloat16
F32 = jnp.float32


def _dot(a, b):
    return jnp.dot(a, b, preferred_element_type=F32)


def _layer_norm(y, g, b):
    mu = jnp.mean(y, axis=-1, keepdims=True)
    yc = y - mu
    var = jnp.mean(yc * yc, axis=-1, keepdims=True)
    return yc * lax.rsqrt(var + LN_EPS) * g + b


def _after(a, done):
    assert a.shape[0] == done.shape[0]
    dep = done[:, 0:LANES]
    for c in range(LANES, done.shape[1], LANES):
        dep = jnp.maximum(dep, done[:, c:c + LANES])
    dep = jnp.concatenate([dep] * (a.shape[1] // LANES), axis=1)
    return jnp.where(pl.program_id(0) >= 0, a, dep)


def _swiglu(xb, win_ref, wo_ref, done):
    acc = None
    rows = xb.shape[0] // len(FF_CHUNKS)
    for k, (c0, cw) in enumerate(FF_CHUNKS):
        gate = _dot(xb, win_ref[:, c0:c0 + cw])
        up = _dot(xb, win_ref[:, D_FF + c0:D_FF + c0 + cw])
        up = jnp.concatenate(
            [_after(up[r0:r0 + rows], done[r0:r0 + rows]) if r0 == k * rows
             else up[r0:r0 + rows] for r0 in range(0, xb.shape[0], rows)], axis=0)
        h = (gate * jax.nn.sigmoid(gate) * up).astype(BF16)
        part = _dot(h, wo_ref[c0:c0 + cw, :])
        acc = part if acc is None else acc + part
    return acc


def _cast_blocks(src_refs, dst_refs):
    for src, dst in zip(src_refs, dst_refs, strict=True):
        dst[...] = src[...].astype(BF16)


def _two_stage(n_tiles, pre_ref, matmul_stage, finish_stage):
    step = pl.program_id(0)

    @pl.when(step == 0)
    def _():
        pre_ref[...] = jnp.zeros_like(pre_ref)

    @pl.when(step < n_tiles)
    def _():
        pre_ref[...] = matmul_stage(finish_stage)

    @pl.when(step == n_tiles)
    def _():
        finish_stage()


def _ffn_ln_kernel(alpha, n_cast, n_tiles, x_ref, win_ref, wo_ref, g_ref, b_ref, *rest):
    cast_src, (o_ref, *cast_dst), pre_ref = rest[:n_cast], rest[n_cast:-1], rest[-1]
    _cast_blocks(cast_src, cast_dst)

    def matmul_stage(finish):
        x = x_ref[...]
        return alpha * x + 0.5 * _swiglu(x.astype(BF16), win_ref, wo_ref, finish())

    def finish_stage():
        out = _layer_norm(pre_ref[...], g_ref[...], b_ref[...])
        o_ref[...] = out
        return out

    _two_stage(n_tiles, pre_ref, matmul_stage, finish_stage)


def _ffn_ple_kernel(alpha, n_tiles, x_ref, p_ref, win_ref, wo_ref, g3_ref, b3_ref,
                    wpi_ref, wpg_ref, g4_ref, b4_ref, o_ref, pre_ref):
    def matmul_stage(finish):
        x = x_ref[...]
        return alpha * x + 0.5 * _swiglu(x.astype(BF16), win_ref, wo_ref, finish())

    def finish_stage():
        x3 = _layer_norm(pre_ref[...], g3_ref[...], b3_ref[...])
        e = _dot(p_ref[...].astype(BF16), wpi_ref[...]) * jax.nn.sigmoid(
            _dot(x3.astype(BF16), wpg_ref[...]))
        out = _layer_norm(alpha * x3 + e, g4_ref[...], b4_ref[...])
        o_ref[...] = out
        return out

    _two_stage(n_tiles, pre_ref, matmul_stage, finish_stage)


def _mixer_kernel(alpha, n_cast, n_tiles, x_ref, win_ref, cw_ref, cb_ref, cwo_ref, lam_ref,
                  bd_ref, cd_ref, dsk_ref, perm_ref, unperm_ref, wglu_ref, wout_ref, g_ref,
                  b_ref, *rest):
    cast_src, o_ref = rest[:n_cast], rest[n_cast]
    cast_dst = rest[n_cast + 1:2 * n_cast + 1]
    s_ref, st_ref, v_ref, pre_ref = rest[2 * n_cast + 1:]
    _cast_blocks(cast_src, cast_dst)
    tl = x_ref.shape[1]
    tm = BATCH * tl
    step = pl.program_id(0)

    @pl.when(step == 0)
    def _():
        st_ref[...] = jnp.zeros_like(st_ref)
        v_ref[:, 0:CONV_PAD, :] = jnp.zeros((BATCH, CONV_PAD, CONV_CH), F32)

    @pl.when(step > 0)
    def _():
        v_ref[:, 0:CONV_PAD, :] = v_ref[:, tl:tl + CONV_PAD, :]

    def finish_stage():
        out = _layer_norm(pre_ref[...], g_ref[...], b_ref[...])
        o_ref[...] = out.reshape(BATCH, tl, D_MODEL)
        return out

    matmul_stage = functools.partial(
        _mixer_matmul_stage, alpha, tl, x_ref, win_ref, cw_ref, cb_ref, cwo_ref, lam_ref,
        bd_ref, cd_ref, dsk_ref, perm_ref, unperm_ref, wglu_ref, wout_ref, s_ref, st_ref, v_ref)
    _two_stage(n_tiles, pre_ref, matmul_stage, finish_stage)


def _mixer_matmul_stage(alpha, tl, x_ref, win_ref, cw_ref, cb_ref, cwo_ref, lam_ref, bd_ref,
                        cd_ref, dsk_ref, perm_ref, unperm_ref, wglu_ref, wout_ref, s_ref,
                        st_ref, v_ref, finish):
    tm = BATCH * tl
    x = x_ref[...].reshape(tm, D_MODEL)
    xb = x.astype(BF16)

    def proj(c0, cw):
        return _dot(xb, win_ref[:, c0:c0 + cw])

    v = proj(CONV_CH, CONV_CH) * proj(2 * CONV_CH, CONV_CH)
    v_ref[:, CONV_PAD:CONV_PAD + tl, :] = v.reshape(BATCH, tl, CONV_CH)
    z = cb_ref[...].reshape(1, 1, CONV_CH)
    for k in range(CONV_K):
        lag = CONV_K - 1 - k
        z = z + cw_ref[k:k + 1, :].reshape(1, 1, CONV_CH) * v_ref[
            :, CONV_PAD - lag:CONV_PAD - lag + tl, :]
    y_conv = _dot((proj(0, CONV_CH) * z.reshape(tm, CONV_CH)).astype(BF16), cwo_ref[...])
    merged = jax.nn.sigmoid(proj(4 * CONV_CH, D_MODEL)) * y_conv

    u = proj(3 * CONV_CH, SSM_WIDTH)
    ub = _dot(perm_ref[...], u.astype(BF16)).astype(BF16)
    for h in range(2):
        s_ref[:, 2 * h * HALF_ST:2 * (h + 1) * HALF_ST] = _dot(
            ub[:, h * HALF_IN:(h + 1) * HALF_IN], bd_ref[h])

    for h in range(2):
        re0 = 2 * h * HALF_ST
        im0 = re0 + HALF_ST
        lam_re = lam_ref[:, re0:re0 + HALF_ST]
        lam_im = lam_ref[:, im0:im0 + HALF_ST]

        def body(l, carry, re0=re0, im0=im0, lam_re=lam_re, lam_im=lam_im):
            sr, si = carry
            r0 = pl.multiple_of(l * SUBLANES, SUBLANES)
            nr = lam_re * sr - lam_im * si + s_ref[pl.ds(r0, SUBLANES), re0:re0 + HALF_ST]
            ni = lam_re * si + lam_im * sr + s_ref[pl.ds(r0, SUBLANES), im0:im0 + HALF_ST]
            s_ref[pl.ds(r0, SUBLANES), re0:re0 + HALF_ST] = nr
            s_ref[pl.ds(r0, SUBLANES), im0:im0 + HALF_ST] = ni
            return nr, ni

        sr, si = lax.fori_loop(
            0, tl, body,
            (st_ref[:, re0:re0 + HALF_ST], st_ref[:, im0:im0 + HALF_ST]), unroll=4)
        st_ref[:, re0:re0 + HALF_ST] = sr
        st_ref[:, im0:im0 + HALF_ST] = si

    ys = []
    for h in range(2):
        sb = s_ref[:, 2 * h * HALF_ST:2 * (h + 1) * HALF_ST].astype(BF16)
        ys.append(_dot(sb, cd_ref[h]))
    y_tb = jnp.concatenate(ys, axis=-1)
    hi = y_tb.astype(BF16)
    lo = (y_tb - hi.astype(F32)).astype(BF16)
    y = _dot(unperm_ref[...], jnp.concatenate([hi, lo], axis=0)) + dsk_ref[...] * u
    sg = jax.nn.gelu(y).astype(BF16)
    y_ssm = _dot(sg, wglu_ref[:, 0:D_MODEL]) * jax.nn.sigmoid(
        _dot(sg, wglu_ref[:, D_MODEL:2 * D_MODEL]))
    merged = merged + jax.nn.sigmoid(proj(4 * CONV_CH + D_MODEL, D_MODEL)) * y_ssm

    merged = _after(merged, finish())
    return alpha * x + _dot(merged.astype(BF16), wout_ref[...])


def _resident(shape):
    zeros = (0,) * len(shape)
    return pl.BlockSpec(shape, lambda i: zeros, pipeline_mode=pl.Buffered(1))


def _matmul_tile(n_tiles):
    return lambda i: jnp.minimum(i, n_tiles - 1)


def _finish_tile(i):
    return jnp.maximum(i - 1, 0)


def _rows(tm, width, tile_of_step):
    return pl.BlockSpec((tm, width), lambda i: (tile_of_step(i), 0))


def _cast_plan(weights, n_tiles):
    specs, shapes = [], []
    tile = _matmul_tile(n_tiles)
    for w in weights:
        rows, cols = w.shape
        n_blocks = n_tiles
        while (rows // n_blocks) % BF16_ROWS or rows % n_blocks:
            n_blocks //= 2
        repeat = n_tiles // n_blocks
        specs.append(pl.BlockSpec((rows // n_blocks, cols),
                                  lambda i, repeat=repeat: (tile(i) // repeat, 0)))
        shapes.append(jax.ShapeDtypeStruct(w.shape, BF16))
    return specs, shapes


def _ffn_ln(x, win, wo, g, b, next_weights, alpha):
    n_tiles = ROWS // TM_FFN
    cast_specs, cast_shapes = _cast_plan(next_weights, n_tiles)
    return pl.pallas_call(
        functools.partial(_ffn_ln_kernel, alpha, len(next_weights), n_tiles),
        grid=(n_tiles + 1,),
        in_specs=[_rows(TM_FFN, D_MODEL, _matmul_tile(n_tiles)), _resident(win.shape),
                  _resident(wo.shape), _resident(g.shape), _resident(b.shape)] + cast_specs,
        out_specs=[_rows(TM_FFN, D_MODEL, _finish_tile)] + cast_specs,
        out_shape=[jax.ShapeDtypeStruct((ROWS, D_MODEL), F32)] + cast_shapes,
        scratch_shapes=[pltpu.VMEM((TM_FFN, D_MODEL), F32)],
        compiler_params=pltpu.CompilerParams(
            dimension_semantics=("arbitrary",), vmem_limit_bytes=VMEM_LIMIT_BYTES),
        name="ffn_ln",
    )(x, win, wo, g, b, *next_weights)


def _ffn_ple(x, p, win, wo, g3, b3, wpi, wpg, g4, b4, alpha):
    n_tiles = ROWS // TM_FFN
    return pl.pallas_call(
        functools.partial(_ffn_ple_kernel, alpha, n_tiles),
        grid=(n_tiles + 1,),
        in_specs=[_rows(TM_FFN, D_MODEL, _matmul_tile(n_tiles)),
                  _rows(TM_FFN, PLE_DIM, _finish_tile), _resident(win.shape),
                  _resident(wo.shape), _resident(g3.shape),
                  _resident(b3.shape), _resident(wpi.shape), _resident(wpg.shape),
                  _resident(g4.shape), _resident(b4.shape)],
        out_specs=_rows(TM_FFN, D_MODEL, _finish_tile),
        out_shape=jax.ShapeDtypeStruct((ROWS, D_MODEL), F32),
        scratch_shapes=[pltpu.VMEM((TM_FFN, D_MODEL), F32)],
        compiler_params=pltpu.CompilerParams(
            dimension_semantics=("arbitrary",), vmem_limit_bytes=VMEM_LIMIT_BYTES),
        name="ffn_ple",
    )(x, p, win, wo, g3, b3, wpi, wpg, g4, b4)


def _mixer(x, win, cw, cb, cwo, lam, bd, cd, dsk, perm, unperm, wglu, wout, g, b,
           next_weights, alpha):
    consts = (win, cw, cb, cwo, lam, bd, cd, dsk, perm, unperm, wglu, wout, g, b)
    n_tiles = SEQ // TL_MIX
    cast_specs, cast_shapes = _cast_plan(next_weights, n_tiles)
    matmul_tile = _matmul_tile(n_tiles)
    block = (BATCH, TL_MIX, D_MODEL)
    return pl.pallas_call(
        functools.partial(_mixer_kernel, alpha, len(next_weights), n_tiles),
        grid=(n_tiles + 1,),
        in_specs=[pl.BlockSpec(block, lambda i: (0, matmul_tile(i), 0))]
        + [_resident(c.shape) for c in consts] + cast_specs,
        out_specs=[pl.BlockSpec(block, lambda i: (0, _finish_tile(i), 0))] + cast_specs,
        out_shape=[jax.ShapeDtypeStruct((BATCH, SEQ, D_MODEL), F32)] + cast_shapes,
        scratch_shapes=[
            pltpu.VMEM((TM_MIX, 4 * HALF_ST), F32),
            pltpu.VMEM((SUBLANES, 4 * HALF_ST), F32),
            pltpu.VMEM((BATCH, CONV_PAD + TL_MIX, CONV_CH), F32),
            pltpu.VMEM((TM_MIX, D_MODEL), F32),
        ],
        compiler_params=pltpu.CompilerParams(
            dimension_semantics=("arbitrary",), vmem_limit_bytes=VMEM_LIMIT_BYTES),
        name="mixer",
    )(x, *consts, *next_weights)


def _block_diag(blocks):
    n, a, b = blocks.shape
    eye = jnp.eye(n, dtype=blocks.dtype)
    return jnp.einsum("gab,gk->gakb", blocks, eye).reshape(n * a, n * b)


def _ssm_matrices(lam_re, lam_im, log_step, b_re, b_im, c_re, c_im):
    dt = jnp.exp(log_step)[:, None]
    mag = jnp.exp(lam_re * dt)
    lbar_re = mag * jnp.cos(lam_im * dt)
    lbar_im = mag * jnp.sin(lam_im * dt)
    den = lam_re * lam_re + lam_im * lam_im
    q_re = ((lbar_re - 1.0) * lam_re + lbar_im * lam_im) / den
    q_im = (lbar_im * lam_re - (lbar_re - 1.0) * lam_im) / den
    bbar_re = q_re[..., None] * b_re - q_im[..., None] * b_im
    bbar_im = q_re[..., None] * b_im + q_im[..., None] * b_re

    lam_cols, bd, cd = [], [], []
    for h in range(2):
        gs = slice(h * HALF_GROUPS, (h + 1) * HALF_GROUPS)
        lam_cols += [lbar_re[gs].reshape(-1), lbar_im[gs].reshape(-1)]
        bd.append(jnp.concatenate(
            [_block_diag(jnp.swapaxes(bbar_re[gs], 1, 2)),
             _block_diag(jnp.swapaxes(bbar_im[gs], 1, 2))], axis=1))
        cd.append(jnp.concatenate(
            [_block_diag(jnp.swapaxes(c_re[gs], 1, 2)),
             _block_diag(jnp.swapaxes(-c_im[gs], 1, 2))], axis=0))
    lam = jnp.broadcast_to(jnp.concatenate(lam_cols)[None, :], (SUBLANES, 4 * HALF_ST))
    return lam, jnp.stack(bd).astype(BF16), jnp.stack(cd).astype(BF16)


def _row_permutations():
    dst = jnp.arange(TM_MIX)
    src = (dst % BATCH) * TL_MIX + dst // BATCH
    perm = (src[:, None] == jnp.arange(TM_MIX)[None, :]).astype(BF16)
    return perm, jnp.concatenate([perm.T, perm.T], axis=1)


def kernel(x, p, ffn1_w_in, ffn1_w_out, ln1_g, ln1_b, mix_w_in, conv_w, conv_b, conv_w_out, ssm_lam_re, ssm_lam_im, ssm_log_step, ssm_b_re, ssm_b_im, ssm_c_re, ssm_c_im, ssm_d, ssm_w_glu, mix_w_out, ln2_g, ln2_b, ffn2_w_in, ffn2_w_out, ln3_g, ln3_b, ple_w_in, ple_w_gate, ln4_g, ln4_b):
    depth = p.shape[0]
    alpha = (2.0 * depth) ** 0.25
    assert x.shape == (BATCH, SEQ, D_MODEL) and BATCH == SUBLANES

    def row(v):
        return v.reshape(1, -1)

    perm, unperm = _row_permutations()
    h = x.reshape(ROWS, D_MODEL)
    for i in range(depth):
        h, mw_in, cw_out, w_glu, mw_out = _ffn_ln(
            h, ffn1_w_in[i].astype(BF16), ffn1_w_out[i].astype(BF16), row(ln1_g[i]),
            row(ln1_b[i]), (mix_w_in[i], conv_w_out[i], ssm_w_glu[i], mix_w_out[i]), alpha)
        lam, bd, cd = _ssm_matrices(ssm_lam_re[i], ssm_lam_im[i], ssm_log_step[i],
                                    ssm_b_re[i], ssm_b_im[i], ssm_c_re[i], ssm_c_im[i])
        h, f2_in, f2_out, pw_in, pw_gate = _mixer(
            h.reshape(BATCH, SEQ, D_MODEL), mw_in, conv_w[i], row(conv_b[i]), cw_out, lam,
            bd, cd, row(ssm_d[i]), perm, unperm, w_glu, mw_out, row(ln2_g[i]), row(ln2_b[i]),
            (ffn2_w_in[i], ffn2_w_out[i], ple_w_in[i], ple_w_gate[i]), alpha)
        h = _ffn_ple(h.reshape(ROWS, D_MODEL), p[i].reshape(ROWS, PLE_DIM), f2_in, f2_out,
                     row(ln3_g[i]), row(ln3_b[i]), pw_in, pw_gate,
                     row(ln4_g[i]), row(ln4_b[i]), alpha)
    return h.reshape(BATCH, SEQ, D_MODEL)
```

```python
import functools

import jax
import jax.numpy as jnp
from jax import lax
from jax.experimental import pallas as pl
from jax.experimental.pallas import tpu as pltpu

D_MODEL = 1024
BATCH = 8
SEQ = 2048
PLE_DIM = 256
D_FF = 2816
CONV_CH = 512
CONV_K = 3
SSM_WIDTH = 512
SSM_GROUP = 16
SSM_GROUPS = SSM_WIDTH // SSM_GROUP
SSM_STATE = 64
LN_EPS = 1e-5

SUBLANES = 8
LANES = 128
BF16_ROWS = 16
VMEM_LIMIT_BYTES = 56 * 1024 * 1024

ROWS = BATCH * SEQ
TM_FFN = 512
TL_MIX = 64
TM_MIX = BATCH * TL_MIX
FF_CHUNKS = ((0, 768), (768, 768), (1536, 768), (2304, 512))
HALF_GROUPS = SSM_GROUPS // 2
HALF_IN = HALF_GROUPS * SSM_GROUP
HALF_ST = HALF_GROUPS * SSM_STATE
CONV_PAD = SUBLANES

BF16 = jnp.bfloat16
F32 = jnp.float32


def _dot(a, b):
    return jnp.dot(a, b, preferred_element_type=F32)


def _layer_norm(y, g, b):
    mu = jnp.mean(y, axis=-1, keepdims=True)
    yc = y - mu
    var = jnp.mean(yc * yc, axis=-1, keepdims=True)
    return yc * lax.rsqrt(var + LN_EPS) * g + b


def _after(a, done):
    assert a.shape[0] == done.shape[0]
    dep = done[:, 0:LANES]
    for c in range(LANES, done.shape[1], LANES):
        dep = jnp.maximum(dep, done[:, c:c + LANES])
    dep = jnp.concatenate([dep] * (a.shape[1] // LANES), axis=1)
    return jnp.where(pl.program_id(0) >= 0, a, dep)


def _swiglu(xb, win_ref, wo_ref, done):
    acc = None
    rows = xb.shape[0] // len(FF_CHUNKS)
    for k, (c0, cw) in enumerate(FF_CHUNKS):
        gate = _dot(xb, win_ref[:, c0:c0 + cw])
        up = _dot(xb, win_ref[:, D_FF + c0:D_FF + c0 + cw])
        up = jnp.concatenate(
            [_after(up[r0:r0 + rows], done[r0:r0 + rows]) if r0 == k * rows
             else up[r0:r0 + rows] for r0 in range(0, xb.shape[0], rows)], axis=0)
        h = (gate * jax.nn.sigmoid(gate) * up).astype(BF16)
        part = _dot(h, wo_ref[c0:c0 + cw, :])
        acc = part if acc is None else acc + part
    return acc


def _cast_blocks(src_refs, dst_refs):
    for src, dst in zip(src_refs, dst_refs, strict=True):
        dst[...] = src[...].astype(BF16)


def _two_stage(n_tiles, pre_ref, matmul_stage, finish_stage):
    step = pl.program_id(0)

    @pl.when(step == 0)
    def _():
        pre_ref[...] = jnp.zeros_like(pre_ref)

    @pl.when(step < n_tiles)
    def _():
        pre_ref[...] = matmul_stage(finish_stage)

    @pl.when(step == n_tiles)
    def _():
        finish_stage()


def _ffn_ln_kernel(alpha, n_cast, n_tiles, x_ref, win_ref, wo_ref, g_ref, b_ref, *rest):
    cast_src, (o_ref, *cast_dst), pre_ref = rest[:n_cast], rest[n_cast:-1], rest[-1]
    _cast_blocks(cast_src, cast_dst)

    def matmul_stage(finish):
        f = _swiglu(x_ref[...].astype(BF16), win_ref, wo_ref, finish())
        return alpha * x_ref[...] + 0.5 * f

    def finish_stage():
        out = _layer_norm(pre_ref[...], g_ref[...], b_ref[...])
        o_ref[...] = out
        return out

    _two_stage(n_tiles, pre_ref, matmul_stage, finish_stage)


def _ffn_ple_kernel(alpha, n_tiles, x_ref, p_ref, win_ref, wo_ref, g3_ref, b3_ref,
                    wpi_ref, wpg_ref, g4_ref, b4_ref, o_ref, pre_ref):
    def matmul_stage(finish):
        f = _swiglu(x_ref[...].astype(BF16), win_ref, wo_ref, finish())
        return alpha * x_ref[...] + 0.5 * f

    def finish_stage():
        x3 = _layer_norm(pre_ref[...], g3_ref[...], b3_ref[...])
        e = _dot(p_ref[...].astype(BF16), wpi_ref[...]) * jax.nn.sigmoid(
            _dot(x3.astype(BF16), wpg_ref[...]))
        out = _layer_norm(alpha * x3 + e, g4_ref[...], b4_ref[...])
        o_ref[...] = out
        return out

    _two_stage(n_tiles, pre_ref, matmul_stage, finish_stage)


def _mixer_kernel(alpha, n_cast, n_tiles, x_ref, win_ref, cw_ref, cb_ref, cwo_ref, lam_ref,
                  bd_ref, cd_ref, dsk_ref, perm_ref, unperm_ref, wglu_ref, wout_ref, g_ref,
                  b_ref, *rest):
    cast_src, o_ref = rest[:n_cast], rest[n_cast]
    cast_dst = rest[n_cast + 1:2 * n_cast + 1]
    s_ref, st_ref, v_ref, pre_ref = rest[2 * n_cast + 1:]
    _cast_blocks(cast_src, cast_dst)
    tl = x_ref.shape[1]
    tm = BATCH * tl
    step = pl.program_id(0)

    @pl.when(step == 0)
    def _():
        st_ref[...] = jnp.zeros_like(st_ref)
        v_ref[:, 0:CONV_PAD, :] = jnp.zeros((BATCH, CONV_PAD, CONV_CH), F32)

    @pl.when(step > 0)
    def _():
        v_ref[:, 0:CONV_PAD, :] = v_ref[:, tl:tl + CONV_PAD, :]

    def finish_stage():
        out = _layer_norm(pre_ref[...], g_ref[...], b_ref[...])
        o_ref[...] = out.reshape(BATCH, tl, D_MODEL)
        return out

    matmul_stage = functools.partial(
        _mixer_matmul_stage, alpha, tl, x_ref, win_ref, cw_ref, cb_ref, cwo_ref, lam_ref,
        bd_ref, cd_ref, dsk_ref, perm_ref, unperm_ref, wglu_ref, wout_ref, s_ref, st_ref, v_ref)
    _two_stage(n_tiles, pre_ref, matmul_stage, finish_stage)


def _mixer_matmul_stage(alpha, tl, x_ref, win_ref, cw_ref, cb_ref, cwo_ref, lam_ref, bd_ref,
                        cd_ref, dsk_ref, perm_ref, unperm_ref, wglu_ref, wout_ref, s_ref,
                        st_ref, v_ref, finish):
    tm = BATCH * tl
    xb = x_ref[...].reshape(tm, D_MODEL).astype(BF16)

    def proj(c0, cw):
        return _dot(xb, win_ref[:, c0:c0 + cw])

    v = proj(CONV_CH, CONV_CH) * proj(2 * CONV_CH, CONV_CH)
    v_ref[:, CONV_PAD:CONV_PAD + tl, :] = v.reshape(BATCH, tl, CONV_CH)
    z = cb_ref[...].reshape(1, 1, CONV_CH)
    for k in range(CONV_K):
        lag = CONV_K - 1 - k
        z = z + cw_ref[k:k + 1, :].reshape(1, 1, CONV_CH) * v_ref[
            :, CONV_PAD - lag:CONV_PAD - lag + tl, :]
    y_conv = _dot((proj(0, CONV_CH) * z.reshape(tm, CONV_CH)).astype(BF16), cwo_ref[...])
    merged = jax.nn.sigmoid(proj(4 * CONV_CH, D_MODEL)) * y_conv

    u = proj(3 * CONV_CH, SSM_WIDTH)
    ub = _dot(perm_ref[...], u.astype(BF16)).astype(BF16)
    for h in range(2):
        s_ref[:, 2 * h * HALF_ST:2 * (h + 1) * HALF_ST] = _dot(
            ub[:, h * HALF_IN:(h + 1) * HALF_IN], bd_ref[h])

    for h in range(2):
        re = slice(2 * h * HALF_ST, (2 * h + 1) * HALF_ST)
        im = slice((2 * h + 1) * HALF_ST, (2 * h + 2) * HALF_ST)
        lam_re, lam_im = lam_ref[:, re], lam_ref[:, im]
        sr, si = st_ref[:, re], st_ref[:, im]
        for l in range(tl):
            rows = slice(l * SUBLANES, (l + 1) * SUBLANES)
            sr, si = (lam_re * sr - lam_im * si + s_ref[rows, re],
                      lam_re * si + lam_im * sr + s_ref[rows, im])
            s_ref[rows, re] = sr
            s_ref[rows, im] = si
        st_ref[:, re] = sr
        st_ref[:, im] = si

    ys = []
    for h in range(2):
        sb = s_ref[:, 2 * h * HALF_ST:2 * (h + 1) * HALF_ST].astype(BF16)
        ys.append(_dot(sb, cd_ref[h]))
    y_tb = jnp.concatenate(ys, axis=-1)
    hi = y_tb.astype(BF16)
    lo = (y_tb - hi.astype(F32)).astype(BF16)
    y = _dot(unperm_ref[...], jnp.concatenate([hi, lo], axis=0)) + dsk_ref[...] * u
    sg = jax.nn.gelu(y).astype(BF16)
    y_ssm = _dot(sg, wglu_ref[:, 0:D_MODEL]) * jax.nn.sigmoid(
        _dot(sg, wglu_ref[:, D_MODEL:2 * D_MODEL]))
    merged = merged + jax.nn.sigmoid(proj(4 * CONV_CH + D_MODEL, D_MODEL)) * y_ssm

    merged = _after(merged, finish())
    return alpha * x_ref[...].reshape(tm, D_MODEL) + _dot(merged.astype(BF16), wout_ref[...])


def _resident(shape):
    zeros = (0,) * len(shape)
    return pl.BlockSpec(shape, lambda i: zeros, pipeline_mode=pl.Buffered(1))


def _matmul_tile(n_tiles):
    return lambda i: jnp.minimum(i, n_tiles - 1)


def _finish_tile(i):
    return jnp.maximum(i - 1, 0)


def _rows(tm, width, tile_of_step):
    return pl.BlockSpec((tm, width), lambda i: (tile_of_step(i), 0))


def _cast_plan(weights, n_tiles):
    specs, shapes = [], []
    tile = _matmul_tile(n_tiles)
    for w in weights:
        rows, cols = w.shape
        n_blocks = n_tiles
        while (rows // n_blocks) % BF16_ROWS or rows % n_blocks:
            n_blocks //= 2
        repeat = n_tiles // n_blocks
        specs.append(pl.BlockSpec((rows // n_blocks, cols),
                                  lambda i, repeat=repeat: (tile(i) // repeat, 0)))
        shapes.append(jax.ShapeDtypeStruct(w.shape, BF16))
    return specs, shapes


def _ffn_ln(x, win, wo, g, b, next_weights, alpha):
    n_tiles = ROWS // TM_FFN
    cast_specs, cast_shapes = _cast_plan(next_weights, n_tiles)
    return pl.pallas_call(
        functools.partial(_ffn_ln_kernel, alpha, len(next_weights), n_tiles),
        grid=(n_tiles + 1,),
        in_specs=[_rows(TM_FFN, D_MODEL, _matmul_tile(n_tiles)), _resident(win.shape),
                  _resident(wo.shape), _resident(g.shape), _resident(b.shape)] + cast_specs,
        out_specs=[_rows(TM_FFN, D_MODEL, _finish_tile)] + cast_specs,
        out_shape=[jax.ShapeDtypeStruct((ROWS, D_MODEL), F32)] + cast_shapes,
        scratch_shapes=[pltpu.VMEM((TM_FFN, D_MODEL), F32)],
        compiler_params=pltpu.CompilerParams(
            dimension_semantics=("arbitrary",), vmem_limit_bytes=VMEM_LIMIT_BYTES),
        name="ffn_ln",
    )(x, win, wo, g, b, *next_weights)


def _ffn_ple(x, p, win, wo, g3, b3, wpi, wpg, g4, b4, alpha):
    n_tiles = ROWS // TM_FFN
    return pl.pallas_call(
        functools.partial(_ffn_ple_kernel, alpha, n_tiles),
        grid=(n_tiles + 1,),
        in_specs=[_rows(TM_FFN, D_MODEL, _matmul_tile(n_tiles)),
                  _rows(TM_FFN, PLE_DIM, _finish_tile), _resident(win.shape),
                  _resident(wo.shape), _resident(g3.shape),
                  _resident(b3.shape), _resident(wpi.shape), _resident(wpg.shape),
                  _resident(g4.shape), _resident(b4.shape)],
        out_specs=_rows(TM_FFN, D_MODEL, _finish_tile),
        out_shape=jax.ShapeDtypeStruct((ROWS, D_MODEL), F32),
        scratch_shapes=[pltpu.VMEM((TM_FFN, D_MODEL), F32)],
        compiler_params=pltpu.CompilerParams(
            dimension_semantics=("arbitrary",), vmem_limit_bytes=VMEM_LIMIT_BYTES),
        name="ffn_ple",
    )(x, p, win, wo, g3, b3, wpi, wpg, g4, b4)


def _mixer(x, win, cw, cb, cwo, lam, bd, cd, dsk, perm, unperm, wglu, wout, g, b,
           next_weights, alpha):
    consts = (win, cw, cb, cwo, lam, bd, cd, dsk, perm, unperm, wglu, wout, g, b)
    n_tiles = SEQ // TL_MIX
    cast_specs, cast_shapes = _cast_plan(next_weights, n_tiles)
    matmul_tile = _matmul_tile(n_tiles)
    block = (BATCH, TL_MIX, D_MODEL)
    return pl.pallas_call(
        functools.partial(_mixer_kernel, alpha, len(next_weights), n_tiles),
        grid=(n_tiles + 1,),
        in_specs=[pl.BlockSpec(block, lambda i: (0, matmul_tile(i), 0))]
        + [_resident(c.shape) for c in consts] + cast_specs,
        out_specs=[pl.BlockSpec(block, lambda i: (0, _finish_tile(i), 0))] + cast_specs,
        out_shape=[jax.ShapeDtypeStruct((BATCH, SEQ, D_MODEL), F32)] + cast_shapes,
        scratch_shapes=[
            pltpu.VMEM((TM_MIX, 4 * HALF_ST), F32),
            pltpu.VMEM((SUBLANES, 4 * HALF_ST), F32),
            pltpu.VMEM((BATCH, CONV_PAD + TL_MIX, CONV_CH), F32),
            pltpu.VMEM((TM_MIX, D_MODEL), F32),
        ],
        compiler_params=pltpu.CompilerParams(
            dimension_semantics=("arbitrary",), vmem_limit_bytes=VMEM_LIMIT_BYTES),
        name="mixer",
    )(x, *consts, *next_weights)


def _block_diag(blocks):
    n, a, b = blocks.shape
    eye = jnp.eye(n, dtype=blocks.dtype)
    return jnp.einsum("gab,gk->gakb", blocks, eye).reshape(n * a, n * b)


def _ssm_matrices(lam_re, lam_im, log_step, b_re, b_im, c_re, c_im):
    dt = jnp.exp(log_step)[:, None]
    mag = jnp.exp(lam_re * dt)
    lbar_re = mag * jnp.cos(lam_im * dt)
    lbar_im = mag * jnp.sin(lam_im * dt)
    den = lam_re * lam_re + lam_im * lam_im
    q_re = ((lbar_re - 1.0) * lam_re + lbar_im * lam_im) / den
    q_im = (lbar_im * lam_re - (lbar_re - 1.0) * lam_im) / den
    bbar_re = q_re[..., None] * b_re - q_im[..., None] * b_im
    bbar_im = q_re[..., None] * b_im + q_im[..., None] * b_re

    lam_cols, bd, cd = [], [], []
    for h in range(2):
        gs = slice(h * HALF_GROUPS, (h + 1) * HALF_GROUPS)
        lam_cols += [lbar_re[gs].reshape(-1), lbar_im[gs].reshape(-1)]
        bd.append(jnp.concatenate(
            [_block_diag(jnp.swapaxes(bbar_re[gs], 1, 2)),
             _block_diag(jnp.swapaxes(bbar_im[gs], 1, 2))], axis=1))
        cd.append(jnp.concatenate(
            [_block_diag(jnp.swapaxes(c_re[gs], 1, 2)),
             _block_diag(jnp.swapaxes(-c_im[gs], 1, 2))], axis=0))
    lam = jnp.broadcast_to(jnp.concatenate(lam_cols)[None, :], (SUBLANES, 4 * HALF_ST))
    return lam, jnp.stack(bd).astype(BF16), jnp.stack(cd).astype(BF16)


def _row_permutations():
    dst = jnp.arange(TM_MIX)
    src = (dst % BATCH) * TL_MIX + dst // BATCH
    perm = (src[:, None] == jnp.arange(TM_MIX)[None, :]).astype(BF16)
    return perm, jnp.concatenate([perm.T, perm.T], axis=1)


def kernel(x, p, ffn1_w_in, ffn1_w_out, ln1_g, ln1_b, mix_w_in, conv_w, conv_b, conv_w_out, ssm_lam_re, ssm_lam_im, ssm_log_step, ssm_b_re, ssm_b_im, ssm_c_re, ssm_c_im, ssm_d, ssm_w_glu, mix_w_out, ln2_g, ln2_b, ffn2_w_in, ffn2_w_out, ln3_g, ln3_b, ple_w_in, ple_w_gate, ln4_g, ln4_b):
    depth = p.shape[0]
    alpha = (2.0 * depth) ** 0.25
    assert x.shape == (BATCH, SEQ, D_MODEL) and BATCH == SUBLANES

    def row(v):
        return v.reshape(1, -1)

    perm, unperm = _row_permutations()
    h = x.reshape(ROWS, D_MODEL)
    for i in range(depth):
        h, mw_in, cw_out, w_glu, mw_out = _ffn_ln(
            h, ffn1_w_in[i].astype(BF16), ffn1_w_out[i].astype(BF16), row(ln1_g[i]),
            row(ln1_b[i]), (mix_w_in[i], conv_w_out[i], ssm_w_glu[i], mix_w_out[i]), alpha)
        lam, bd, cd = _ssm_matrices(ssm_lam_re[i], ssm_lam_im[i], ssm_log_step[i],
                                    ssm_b_re[i], ssm_b_im[i], ssm_c_re[i], ssm_c_im[i])
        h, f2_in, f2_out, pw_in, pw_gate = _mixer(
            h.reshape(BATCH, SEQ, D_MODEL), mw_in, conv_w[i], row(conv_b[i]), cw_out, lam,
            bd, cd, row(ssm_d[i]), perm, unperm, w_glu, mw_out, row(ln2_g[i]), row(ln2_b[i]),
            (ffn2_w_in[i], ffn2_w_out[i], ple_w_in[i], ple_w_gate[i]), alpha)
        h = _ffn_ple(h.reshape(ROWS, D_MODEL), p[i].reshape(ROWS, PLE_DIM), f2_in, f2_out,
                     row(ln3_g[i]), row(ln3_b[i]), pw_in, pw_gate,
                     row(ln4_g[i]), row(ln4_b[i]), alpha)
    return h.reshape(BATCH, SEQ, D_MODEL)
```

```python
import functools

import jax
import jax.numpy as jnp
import numpy as np
from jax import lax
from jax.experimental import pallas as pl
from jax.experimental.pallas import tpu as pltpu

D_MODEL = 1024
BATCH = 8
SEQ = 2048
PLE_DIM = 256
D_FF = 2816
CONV_CH = 512
CONV_K = 3
SSM_WIDTH = 512
SSM_GROUP = 16
SSM_GROUPS = SSM_WIDTH // SSM_GROUP
SSM_STATE = 64
LN_EPS = 1e-5

SUBLANES = 8
LANES = 128
BF16_ROWS = 16
VMEM_LIMIT_BYTES = 56 * 1024 * 1024

ROWS = BATCH * SEQ
TM_FFN = 512
TL_MIX = 64
TM_MIX = BATCH * TL_MIX
FF_CHUNKS = ((0, 768), (768, 768), (1536, 768), (2304, 512))
FF_ANCHORS = 4
HALF_GROUPS = SSM_GROUPS // 2
HALF_IN = HALF_GROUPS * SSM_GROUP
HALF_ST = HALF_GROUPS * SSM_STATE
CONV_PAD = SUBLANES
MXU_DIM = 256
PERM_ROWS = MXU_DIM
PERM_TL = PERM_ROWS // BATCH

BF16 = jnp.bfloat16
F32 = jnp.float32


def _dot(a, b):
    return jnp.dot(a, b, preferred_element_type=F32)


def _layer_norm(y, g, b):
    mu = jnp.mean(y, axis=-1, keepdims=True)
    yc = y - mu
    var = jnp.mean(yc * yc, axis=-1, keepdims=True)
    return yc * lax.rsqrt(var + LN_EPS) * g + b


def _after(a, done):
    assert a.shape[0] == done.shape[0]
    dep = done[:, 0:LANES]
    for c in range(LANES, done.shape[1], LANES):
        dep = jnp.maximum(dep, done[:, c:c + LANES])
    dep = jnp.concatenate([dep] * (a.shape[1] // LANES), axis=1)
    return jnp.where(pl.program_id(0) >= 0, a, dep)


def _swiglu(xb, win_ref, wo_ref, done):
    acc = None
    rows = xb.shape[0] // FF_ANCHORS
    for k, (c0, cw) in enumerate(FF_CHUNKS):
        gate = _dot(xb, win_ref[:, c0:c0 + cw])
        up = _dot(xb, win_ref[:, D_FF + c0:D_FF + c0 + cw])
        up = jnp.concatenate(
            [_after(up[r0:r0 + rows], done[r0:r0 + rows]) if r0 == k * rows
             else up[r0:r0 + rows] for r0 in range(0, xb.shape[0], rows)], axis=0)
        h = (gate * jax.nn.sigmoid(gate) * up).astype(BF16)
        part = _dot(h, wo_ref[c0:c0 + cw, :])
        acc = part if acc is None else acc + part
    return acc


def _time_blocks(a, tl):
    return [jnp.concatenate([a[b * tl + t0:b * tl + t0 + PERM_TL] for b in range(BATCH)], axis=0)
            for t0 in range(0, tl, PERM_TL)]


def _cast_blocks(src_refs, dst_refs):
    for src, dst in zip(src_refs, dst_refs, strict=True):
        dst[...] = src[...].astype(BF16)


def _two_stage(n_tiles, pre_ref, matmul_stage, finish_stage):
    step = pl.program_id(0)

    @pl.when(step == 0)
    def _():
        pre_ref[...] = jnp.zeros_like(pre_ref)

    @pl.when(step < n_tiles)
    def _():
        pre_ref[...] = matmul_stage(finish_stage)

    @pl.when(step == n_tiles)
    def _():
        finish_stage()


def _ffn_ln_kernel(alpha, n_cast, n_tiles, x_ref, win_ref, wo_ref, g_ref, b_ref, *rest):
    cast_src, (o_ref, *cast_dst), pre_ref = rest[:n_cast], rest[n_cast:-1], rest[-1]

    def matmul_stage(finish):
        _cast_blocks(cast_src, cast_dst)
        f = _swiglu(x_ref[...].astype(BF16), win_ref, wo_ref, finish())
        return alpha * x_ref[...] + 0.5 * f

    def finish_stage():
        out = _layer_norm(pre_ref[...], g_ref[...], b_ref[...])
        o_ref[...] = out
        return out

    _two_stage(n_tiles, pre_ref, matmul_stage, finish_stage)


def _ffn_ple_kernel(alpha, n_tiles, x_ref, p_ref, win_ref, wo_ref, g3_ref, b3_ref,
                    wpi_ref, wpg_ref, g4_ref, b4_ref, o_ref, pre_ref):
    def matmul_stage(finish):
        f = _swiglu(x_ref[...].astype(BF16), win_ref, wo_ref, finish())
        return alpha * x_ref[...] + 0.5 * f

    def finish_stage():
        x3 = _layer_norm(pre_ref[...], g3_ref[...], b3_ref[...])
        e = _dot(p_ref[...].astype(BF16), wpi_ref[...]) * jax.nn.sigmoid(
            _dot(x3.astype(BF16), wpg_ref[...]))
        out = _layer_norm(alpha * x3 + e, g4_ref[...], b4_ref[...])
        o_ref[...] = out
        return out

    _two_stage(n_tiles, pre_ref, matmul_stage, finish_stage)


def _mixer_kernel(alpha, n_cast, n_tiles, x_ref, win_ref, cw_ref, cb_ref, cwo_ref, lam_ref,
                  bd_ref, cd_ref, dsk_ref, perm_ref, unperm_ref, wglu_ref, wout_ref, g_ref,
                  b_ref, *rest):
    cast_src, o_ref = rest[:n_cast], rest[n_cast]
    cast_dst = rest[n_cast + 1:2 * n_cast + 1]
    s_ref, st_ref, v_ref, pre_ref = rest[2 * n_cast + 1:]
    tl = x_ref.shape[1]
    tm = BATCH * tl
    step = pl.program_id(0)

    @pl.when(step == 0)
    def _():
        st_ref[...] = jnp.zeros_like(st_ref)
        v_ref[:, 0:CONV_PAD, :] = jnp.zeros((BATCH, CONV_PAD, CONV_CH), F32)

    def finish_stage():
        out = _layer_norm(pre_ref[...], g_ref[...], b_ref[...])
        o_ref[...] = out.reshape(BATCH, tl, D_MODEL)
        return out

    def matmul_stage(finish):
        _cast_blocks(cast_src, cast_dst)
        return _mixer_matmul_stage(
            alpha, tl, x_ref, win_ref, cw_ref, cb_ref, cwo_ref, lam_ref, bd_ref, cd_ref,
            dsk_ref, perm_ref, unperm_ref, wglu_ref, wout_ref, s_ref, st_ref, v_ref, finish)

    _two_stage(n_tiles, pre_ref, matmul_stage, finish_stage)


def _mixer_matmul_stage(alpha, tl, x_ref, win_ref, cw_ref, cb_ref, cwo_ref, lam_ref, bd_ref,
                        cd_ref, dsk_ref, perm_ref, unperm_ref, wglu_ref, wout_ref, s_ref,
                        st_ref, v_ref, finish):
    tm = BATCH * tl
    xb = x_ref[...].reshape(tm, D_MODEL).astype(BF16)

    def proj(c0, cw):
        return _dot(xb, win_ref[:, c0:c0 + cw])

    v = proj(CONV_CH, CONV_CH) * proj(2 * CONV_CH, CONV_CH)
    v_ref[:, CONV_PAD:CONV_PAD + tl, :] = v.reshape(BATCH, tl, CONV_CH)
    z = cb_ref[...].reshape(1, 1, CONV_CH)
    for k in range(CONV_K):
        lag = CONV_K - 1 - k
        z = z + cw_ref[k:k + 1, :].reshape(1, 1, CONV_CH) * v_ref[
            :, CONV_PAD - lag:CONV_PAD - lag + tl, :]
    v_ref[:, 0:CONV_PAD, :] = v_ref[:, tl:tl + CONV_PAD, :]
    y_conv = _dot((proj(0, CONV_CH) * z.reshape(tm, CONV_CH)).astype(BF16), cwo_ref[...])
    merged = jax.nn.sigmoid(proj(4 * CONV_CH, D_MODEL)) * y_conv

    u = proj(3 * CONV_CH, SSM_WIDTH)
    ub = jnp.concatenate([_dot(perm_ref[...], blk.astype(BF16)).astype(BF16)
                          for blk in _time_blocks(u, tl)], axis=0)
    for h in range(2):
        s_ref[:, 2 * h * HALF_ST:2 * (h + 1) * HALF_ST] = _dot(
            ub[:, h * HALF_IN:(h + 1) * HALF_IN], bd_ref[h])

    for h in range(2):
        re = slice(2 * h * HALF_ST, (2 * h + 1) * HALF_ST)
        im = slice((2 * h + 1) * HALF_ST, (2 * h + 2) * HALF_ST)
        lam_re, lam_im = lam_ref[:, re], lam_ref[:, im]
        sr, si = st_ref[:, re], st_ref[:, im]
        for l in range(tl):
            rows = slice(l * SUBLANES, (l + 1) * SUBLANES)
            sr, si = (lam_re * sr - lam_im * si + s_ref[rows, re],
                      lam_re * si + lam_im * sr + s_ref[rows, im])
            s_ref[rows, re] = sr
            s_ref[rows, im] = si
        st_ref[:, re] = sr
        st_ref[:, im] = si

    ys = []
    for h in range(2):
        sb = s_ref[:, 2 * h * HALF_ST:2 * (h + 1) * HALF_ST].astype(BF16)
        ys.append(_dot(sb, cd_ref[h]))
    y_tb = jnp.concatenate(ys, axis=-1)
    hi = y_tb.astype(BF16)
    lo = (y_tb - hi.astype(F32)).astype(BF16)
    blocks = [_dot(unperm_ref[...], jnp.concatenate([hi[r0:r0 + PERM_ROWS],
                                                     lo[r0:r0 + PERM_ROWS]], axis=0))
              for r0 in range(0, tm, PERM_ROWS)]
    y = jnp.concatenate([blk[b * PERM_TL:(b + 1) * PERM_TL]
                         for b in range(BATCH) for blk in blocks], axis=0) + dsk_ref[...] * u
    sg = jax.nn.gelu(y).astype(BF16)
    y_ssm = _dot(sg, wglu_ref[:, 0:D_MODEL]) * jax.nn.sigmoid(
        _dot(sg, wglu_ref[:, D_MODEL:2 * D_MODEL]))
    merged = merged + jax.nn.sigmoid(proj(4 * CONV_CH + D_MODEL, D_MODEL)) * y_ssm

    merged = _after(merged, finish())
    return alpha * x_ref[...].reshape(tm, D_MODEL) + _dot(merged.astype(BF16), wout_ref[...])


def _resident(shape):
    zeros = (0,) * len(shape)
    return pl.BlockSpec(shape, lambda i: zeros, pipeline_mode=pl.Buffered(1))


def _matmul_tile(n_tiles):
    return lambda i: jnp.minimum(i, n_tiles - 1)


def _finish_tile(i):
    return jnp.maximum(i - 1, 0)


def _rows(tm, width, tile_of_step):
    return pl.BlockSpec((tm, width), lambda i: (tile_of_step(i), 0))


def _cast_plan(weights, n_tiles):
    specs, shapes = [], []
    tile = _matmul_tile(n_tiles)
    for w in weights:
        rows, cols = w.shape
        n_blocks = n_tiles
        while (rows // n_blocks) % BF16_ROWS or rows % n_blocks:
            n_blocks //= 2
        repeat = n_tiles // n_blocks
        specs.append(pl.BlockSpec((rows // n_blocks, cols),
                                  lambda i, repeat=repeat: (tile(i) // repeat, 0)))
        shapes.append(jax.ShapeDtypeStruct(w.shape, BF16))
    return specs, shapes


def _ffn_ln(x, win, wo, g, b, next_weights, alpha):
    n_tiles = ROWS // TM_FFN
    cast_specs, cast_shapes = _cast_plan(next_weights, n_tiles)
    return pl.pallas_call(
        functools.partial(_ffn_ln_kernel, alpha, len(next_weights), n_tiles),
        grid=(n_tiles + 1,),
        in_specs=[_rows(TM_FFN, D_MODEL, _matmul_tile(n_tiles)), _resident(win.shape),
                  _resident(wo.shape), _resident(g.shape), _resident(b.shape)] + cast_specs,
        out_specs=[_rows(TM_FFN, D_MODEL, _finish_tile)] + cast_specs,
        out_shape=[jax.ShapeDtypeStruct((ROWS, D_MODEL), F32)] + cast_shapes,
        scratch_shapes=[pltpu.VMEM((TM_FFN, D_MODEL), F32)],
        compiler_params=pltpu.CompilerParams(
            dimension_semantics=("arbitrary",), vmem_limit_bytes=VMEM_LIMIT_BYTES),
        name="ffn_ln",
    )(x, win, wo, g, b, *next_weights)


def _ffn_ple(x, p, win, wo, g3, b3, wpi, wpg, g4, b4, alpha):
    n_tiles = ROWS // TM_FFN
    return pl.pallas_call(
        functools.partial(_ffn_ple_kernel, alpha, n_tiles),
        grid=(n_tiles + 1,),
        in_specs=[_rows(TM_FFN, D_MODEL, _matmul_tile(n_tiles)),
                  _rows(TM_FFN, PLE_DIM, _finish_tile), _resident(win.shape),
                  _resident(wo.shape), _resident(g3.shape),
                  _resident(b3.shape), _resident(wpi.shape), _resident(wpg.shape),
                  _resident(g4.shape), _resident(b4.shape)],
        out_specs=_rows(TM_FFN, D_MODEL, _finish_tile),
        out_shape=jax.ShapeDtypeStruct((ROWS, D_MODEL), F32),
        scratch_shapes=[pltpu.VMEM((TM_FFN, D_MODEL), F32)],
        compiler_params=pltpu.CompilerParams(
            dimension_semantics=("arbitrary",), vmem_limit_bytes=VMEM_LIMIT_BYTES),
        name="ffn_ple",
    )(x, p, win, wo, g3, b3, wpi, wpg, g4, b4)


def _mixer(x, win, cw, cb, cwo, lam, bd, cd, dsk, perm, unperm, wglu, wout, g, b,
           next_weights, alpha):
    consts = (win, cw, cb, cwo, lam, bd, cd, dsk, perm, unperm, wglu, wout, g, b)
    n_tiles = SEQ // TL_MIX
    cast_specs, cast_shapes = _cast_plan(next_weights, n_tiles)
    matmul_tile = _matmul_tile(n_tiles)
    block = (BATCH, TL_MIX, D_MODEL)
    return pl.pallas_call(
        functools.partial(_mixer_kernel, alpha, len(next_weights), n_tiles),
        grid=(n_tiles + 1,),
        in_specs=[pl.BlockSpec(block, lambda i: (0, matmul_tile(i), 0))]
        + [_resident(c.shape) for c in consts] + cast_specs,
        out_specs=[pl.BlockSpec(block, lambda i: (0, _finish_tile(i), 0))] + cast_specs,
        out_shape=[jax.ShapeDtypeStruct((BATCH, SEQ, D_MODEL), F32)] + cast_shapes,
        scratch_shapes=[
            pltpu.VMEM((TM_MIX, 4 * HALF_ST), F32),
            pltpu.VMEM((SUBLANES, 4 * HALF_ST), F32),
            pltpu.VMEM((BATCH, CONV_PAD + TL_MIX, CONV_CH), F32),
            pltpu.VMEM((TM_MIX, D_MODEL), F32),
        ],
        compiler_params=pltpu.CompilerParams(
            dimension_semantics=("arbitrary",), vmem_limit_bytes=VMEM_LIMIT_BYTES),
        name="mixer",
    )(x, *consts, *next_weights)


def _block_diag(blocks):
    h, n, a, b = blocks.shape
    eye = jnp.eye(n, dtype=blocks.dtype)
    return (blocks[:, :, :, None, :] * eye[None, :, None, :, None]).reshape(h, n * a, n * b)


def _ssm_matrices(lam_re, lam_im, log_step, b_re, b_im, c_re, c_im):
    dt = jnp.exp(log_step)[:, None]
    mag = jnp.exp(lam_re * dt)
    lbar_re = mag * jnp.cos(lam_im * dt)
    lbar_im = mag * jnp.sin(lam_im * dt)
    den = lam_re * lam_re + lam_im * lam_im
    q_re = ((lbar_re - 1.0) * lam_re + lbar_im * lam_im) / den
    q_im = (lbar_im * lam_re - (lbar_re - 1.0) * lam_im) / den
    bbar_re = q_re[..., None] * b_re - q_im[..., None] * b_im
    bbar_im = q_re[..., None] * b_im + q_im[..., None] * b_re

    def halves(a):
        return a.reshape(2, HALF_GROUPS, *a.shape[1:])

    def halves_t(a):
        return jnp.swapaxes(halves(a), 2, 3)

    lam = jnp.stack([halves(lbar_re), halves(lbar_im)], axis=1).reshape(1, 4 * HALF_ST)
    lam = jnp.broadcast_to(lam, (SUBLANES, 4 * HALF_ST))
    bd = jnp.concatenate([_block_diag(halves_t(bbar_re)),
                          _block_diag(halves_t(bbar_im))], axis=2)
    cd = jnp.concatenate([_block_diag(halves_t(c_re)),
                          _block_diag(halves_t(-c_im))], axis=1)
    return lam, bd.astype(BF16), cd.astype(BF16)


def _row_permutations():
    dst = np.arange(PERM_ROWS)
    src = (dst % BATCH) * PERM_TL + dst // BATCH
    perm = (src[:, None] == np.arange(PERM_ROWS)[None, :]).astype(np.float32)
    return (jnp.asarray(perm, dtype=BF16),
            jnp.asarray(np.concatenate([perm.T, perm.T], axis=1), dtype=BF16))


def kernel(x, p, ffn1_w_in, ffn1_w_out, ln1_g, ln1_b, mix_w_in, conv_w, conv_b, conv_w_out, ssm_lam_re, ssm_lam_im, ssm_log_step, ssm_b_re, ssm_b_im, ssm_c_re, ssm_c_im, ssm_d, ssm_w_glu, mix_w_out, ln2_g, ln2_b, ffn2_w_in, ffn2_w_out, ln3_g, ln3_b, ple_w_in, ple_w_gate, ln4_g, ln4_b):
    depth = p.shape[0]
    alpha = (2.0 * depth) ** 0.25
    assert x.shape == (BATCH, SEQ, D_MODEL) and BATCH == SUBLANES

    def row(v, i):
        return v[i:i + 1].reshape(1, -1)

    perm, unperm = _row_permutations()
    h = x.reshape(ROWS, D_MODEL)
    for i in range(depth):
        h, mw_in, cw_out, w_glu, mw_out = _ffn_ln(
            h, ffn1_w_in[i].astype(BF16), ffn1_w_out[i].astype(BF16), row(ln1_g, i),
            row(ln1_b, i), (mix_w_in[i], conv_w_out[i], ssm_w_glu[i], mix_w_out[i]), alpha)
        lam, bd, cd = _ssm_matrices(ssm_lam_re[i], ssm_lam_im[i], ssm_log_step[i],
                                    ssm_b_re[i], ssm_b_im[i], ssm_c_re[i], ssm_c_im[i])
        h, f2_in, f2_out, pw_in, pw_gate = _mixer(
            h.reshape(BATCH, SEQ, D_MODEL), mw_in, conv_w[i], row(conv_b, i), cw_out, lam,
            bd, cd, row(ssm_d, i), perm, unperm, w_glu, mw_out, row(ln2_g, i), row(ln2_b, i),
            (ffn2_w_in[i], ffn2_w_out[i], ple_w_in[i], ple_w_gate[i]), alpha)
        h = _ffn_ple(h.reshape(ROWS, D_MODEL), p[i].reshape(ROWS, PLE_DIM), f2_in, f2_out,
                     row(ln3_g, i), row(ln3_b, i), pw_in, pw_gate,
                     row(ln4_g, i), row(ln4_b, i), alpha)
    return h.reshape(BATCH, SEQ, D_MODEL)
```

```python
import functools

import jax
import jax.numpy as jnp
import numpy as np
from jax import lax
from jax.experimental import pallas as pl
from jax.experimental.pallas import tpu as pltpu

D_MODEL = 1024
BATCH = 8
SEQ = 2048
PLE_DIM = 256
D_FF = 2816
CONV_CH = 512
CONV_K = 3
SSM_WIDTH = 512
SSM_GROUP = 16
SSM_GROUPS = SSM_WIDTH // SSM_GROUP
SSM_STATE = 64
LN_EPS = 1e-5

SUBLANES = 8
LANES = 128
BF16_ROWS = 16
VMEM_LIMIT_BYTES = 56 * 1024 * 1024

ROWS = BATCH * SEQ
TM_FFN = 512
TL_MIX = 64
TM_MIX = BATCH * TL_MIX
FF_CHUNKS = ((0, 768), (768, 768), (1536, 768), (2304, 512))
FF_ANCHORS = 4
HALF_GROUPS = SSM_GROUPS // 2
HALF_IN = HALF_GROUPS * SSM_GROUP
HALF_ST = HALF_GROUPS * SSM_STATE
CONV_PAD = SUBLANES
MXU_DIM = 256
PERM_ROWS = MXU_DIM
PERM_TL = PERM_ROWS // BATCH

BF16 = jnp.bfloat16
F32 = jnp.float32


def _dot(a, b):
    return jnp.dot(a, b, preferred_element_type=F32)


def _layer_norm(y, g, b):
    mu = jnp.mean(y, axis=-1, keepdims=True)
    yc = y - mu
    var = jnp.mean(yc * yc, axis=-1, keepdims=True)
    return yc * lax.rsqrt(var + LN_EPS) * g + b


def _after(a, done):
    assert a.shape[0] == done.shape[0]
    dep = done[:, 0:LANES]
    for c in range(LANES, done.shape[1], LANES):
        dep = jnp.maximum(dep, done[:, c:c + LANES])
    dep = jnp.concatenate([dep] * (a.shape[1] // LANES), axis=1)
    return jnp.where(pl.program_id(0) >= 0, a, dep)


def _swiglu(xb, win_ref, wo_ref, done):
    acc = None
    rows = xb.shape[0] // FF_ANCHORS
    for k, (c0, cw) in enumerate(FF_CHUNKS):
        gate = _dot(xb, win_ref[:, c0:c0 + cw])
        up = _dot(xb, win_ref[:, D_FF + c0:D_FF + c0 + cw])
        up = jnp.concatenate(
            [_after(up[r0:r0 + rows], done[r0:r0 + rows]) if r0 == k * rows
             else up[r0:r0 + rows] for r0 in range(0, xb.shape[0], rows)], axis=0)
        h = (gate * jax.nn.sigmoid(gate) * up).astype(BF16)
        part = _dot(h, wo_ref[c0:c0 + cw, :])
        acc = part if acc is None else acc + part
    return acc


def _time_blocks(a, tl):
    return [jnp.concatenate([a[b * tl + t0:b * tl + t0 + PERM_TL] for b in range(BATCH)], axis=0)
            for t0 in range(0, tl, PERM_TL)]


def _cast_blocks(src_refs, dst_refs):
    for src, dst in zip(src_refs, dst_refs, strict=True):
        dst[...] = src[...].astype(BF16)


def _two_stage(n_tiles, pre_ref, matmul_stage, finish_stage):
    step = pl.program_id(0)

    @pl.when(step == 0)
    def _():
        pre_ref[...] = jnp.zeros_like(pre_ref)

    @pl.when(step < n_tiles)
    def _():
        pre_ref[...] = matmul_stage(finish_stage)

    @pl.when(step == n_tiles)
    def _():
        finish_stage()


def _ffn_ln_kernel(alpha, n_cast, n_tiles, x_ref, win_ref, wo_ref, g_ref, b_ref, *rest):
    cast_src, (o_ref, *cast_dst), pre_ref = rest[:n_cast], rest[n_cast:-1], rest[-1]

    def matmul_stage(finish):
        _cast_blocks(cast_src, cast_dst)
        f = _swiglu(x_ref[...].astype(BF16), win_ref, wo_ref, finish())
        return alpha * x_ref[...] + 0.5 * f

    def finish_stage():
        out = _layer_norm(pre_ref[...], g_ref[...], b_ref[...])
        o_ref[...] = out
        return out

    _two_stage(n_tiles, pre_ref, matmul_stage, finish_stage)


def _ffn_ple_kernel(alpha, n_tiles, x_ref, p_ref, win_ref, wo_ref, g3_ref, b3_ref,
                    wpi_ref, wpg_ref, g4_ref, b4_ref, o_ref, pre_ref):
    def matmul_stage(finish):
        f = _swiglu(x_ref[...].astype(BF16), win_ref, wo_ref, finish())
        return alpha * x_ref[...] + 0.5 * f

    def finish_stage():
        x3 = _layer_norm(pre_ref[...], g3_ref[...], b3_ref[...])
        e = _dot(p_ref[...].astype(BF16), wpi_ref[...]) * jax.nn.sigmoid(
            _dot(x3.astype(BF16), wpg_ref[...]))
        out = _layer_norm(alpha * x3 + e, g4_ref[...], b4_ref[...])
        o_ref[...] = out
        return out

    _two_stage(n_tiles, pre_ref, matmul_stage, finish_stage)


def _mixer_kernel(alpha, n_cast, n_tiles, x_ref, win_ref, cw_ref, cb_ref, cwo_ref, lam_ref,
                  bd_ref, cd_ref, dsk_ref, perm_ref, unperm_ref, wglu_ref, wout_ref, g_ref,
                  b_ref, *rest):
    cast_src, o_ref = rest[:n_cast], rest[n_cast]
    cast_dst = rest[n_cast + 1:2 * n_cast + 1]
    s_ref, st_ref, v_ref, pre_ref = rest[2 * n_cast + 1:]
    tl = x_ref.shape[1]
    tm = BATCH * tl
    step = pl.program_id(0)

    @pl.when(step == 0)
    def _():
        st_ref[...] = jnp.zeros_like(st_ref)
        v_ref[:, 0:CONV_PAD, :] = jnp.zeros((BATCH, CONV_PAD, CONV_CH), F32)

    def finish_stage():
        out = _layer_norm(pre_ref[...], g_ref[...], b_ref[...])
        o_ref[...] = out.reshape(BATCH, tl, D_MODEL)
        return out

    def matmul_stage(finish):
        _cast_blocks(cast_src, cast_dst)
        return _mixer_matmul_stage(
            alpha, tl, x_ref, win_ref, cw_ref, cb_ref, cwo_ref, lam_ref, bd_ref, cd_ref,
            dsk_ref, perm_ref, unperm_ref, wglu_ref, wout_ref, s_ref, st_ref, v_ref, finish)

    _two_stage(n_tiles, pre_ref, matmul_stage, finish_stage)


def _mixer_matmul_stage(alpha, tl, x_ref, win_ref, cw_ref, cb_ref, cwo_ref, lam_ref, bd_ref,
                        cd_ref, dsk_ref, perm_ref, unperm_ref, wglu_ref, wout_ref, s_ref,
                        st_ref, v_ref, finish):
    tm = BATCH * tl
    xb = x_ref[...].reshape(tm, D_MODEL).astype(BF16)

    def proj(c0, cw):
        return _dot(xb, win_ref[:, c0:c0 + cw])

    n_blocks = tl // PERM_TL
    cols = [(slice(2 * h * HALF_ST, (2 * h + 1) * HALF_ST),
             slice((2 * h + 1) * HALF_ST, (2 * h + 2) * HALF_ST)) for h in range(2)]
    block_rows = [slice(j * PERM_ROWS, (j + 1) * PERM_ROWS) for j in range(n_blocks)]

    u_blocks = _time_blocks(proj(3 * CONV_CH, SSM_WIDTH), tl)
    conv_c = proj(CONV_CH, CONV_CH)
    ub_blocks = [_dot(perm_ref[...], u.astype(BF16)).astype(BF16) for u in u_blocks]

    v = conv_c * proj(2 * CONV_CH, CONV_CH)
    v_ref[:, CONV_PAD:CONV_PAD + tl, :] = v.reshape(BATCH, tl, CONV_CH)
    z = cb_ref[...].reshape(1, 1, CONV_CH)
    for k in range(CONV_K):
        lag = CONV_K - 1 - k
        z = z + cw_ref[k:k + 1, :].reshape(1, 1, CONV_CH) * v_ref[
            :, CONV_PAD - lag:CONV_PAD - lag + tl, :]
    v_ref[:, 0:CONV_PAD, :] = v_ref[:, tl:tl + CONV_PAD, :]

    for rows, ub in zip(block_rows, ub_blocks, strict=True):
        for h in range(2):
            s_ref[rows, 2 * h * HALF_ST:2 * (h + 1) * HALF_ST] = _dot(
                ub[:, h * HALF_IN:(h + 1) * HALF_IN], bd_ref[h])
    conv_gated = (proj(0, CONV_CH) * z.reshape(tm, CONV_CH)).astype(BF16)

    def scan_block(j, state):
        out = []
        for (re, im), (sr, si) in zip(cols, state, strict=True):
            lam_re, lam_im = lam_ref[:, re], lam_ref[:, im]
            for l in range(j * PERM_TL, (j + 1) * PERM_TL):
                rows = slice(l * SUBLANES, (l + 1) * SUBLANES)
                sr, si = (lam_re * sr - lam_im * si + s_ref[rows, re],
                          lam_re * si + lam_im * sr + s_ref[rows, im])
                s_ref[rows, re] = sr
                s_ref[rows, im] = si
            out.append((sr, si))
        return out

    def c_proj(j):
        return jnp.concatenate(
            [_dot(s_ref[block_rows[j], 2 * h * HALF_ST:2 * (h + 1) * HALF_ST].astype(BF16),
                  cd_ref[h]) for h in range(2)], axis=-1)

    fillers = [lambda: _dot(conv_gated, cwo_ref[...]),
               lambda: proj(4 * CONV_CH, D_MODEL),
               lambda: proj(4 * CONV_CH + D_MODEL, D_MODEL)]
    filled = []
    state = [(st_ref[:, re], st_ref[:, im]) for re, im in cols]
    y_tb_blocks = []
    for j in range(n_blocks):
        state = scan_block(j, state)
        take = -(-len(fillers) // (n_blocks - j))
        filled += [f() for f in fillers[:take]]
        fillers = fillers[take:]
        y_tb_blocks.append(c_proj(j))
    y_conv, gate_conv, gate_ssm = filled
    for (re, im), (sr, si) in zip(cols, state, strict=True):
        st_ref[:, re] = sr
        st_ref[:, im] = si

    y_blocks = []
    for y_tb, u in zip(y_tb_blocks, u_blocks, strict=True):
        hi = y_tb.astype(BF16)
        lo = (y_tb - hi.astype(F32)).astype(BF16)
        y_blocks.append(
            _dot(unperm_ref[...], jnp.concatenate([hi, lo], axis=0)) + dsk_ref[...] * u)
    merged_blocks = _time_blocks(jax.nn.sigmoid(gate_conv) * y_conv, tl)
    gate_blocks = _time_blocks(gate_ssm, tl)
    sg_blocks = [jax.nn.gelu(y).astype(BF16) for y in y_blocks]
    glu_a = [_dot(sg, wglu_ref[:, 0:D_MODEL]) for sg in sg_blocks]
    glu_b = [_dot(sg, wglu_ref[:, D_MODEL:2 * D_MODEL]) for sg in sg_blocks]
    done_blocks = _time_blocks(finish(), tl)
    pre_blocks = []
    for j in range(n_blocks):
        m = merged_blocks[j] + jax.nn.sigmoid(gate_blocks[j]) * (
            glu_a[j] * jax.nn.sigmoid(glu_b[j]))
        m = _after(m, done_blocks[j])
        x = x_ref[:, j * PERM_TL:(j + 1) * PERM_TL, :].reshape(PERM_ROWS, D_MODEL)
        pre_blocks.append(alpha * x + _dot(m.astype(BF16), wout_ref[...]))
    return jnp.concatenate([blk[b * PERM_TL:(b + 1) * PERM_TL]
                            for b in range(BATCH) for blk in pre_blocks], axis=0)


def _resident(shape):
    zeros = (0,) * len(shape)
    return pl.BlockSpec(shape, lambda i: zeros, pipeline_mode=pl.Buffered(1))


def _matmul_tile(n_tiles):
    return lambda i: jnp.minimum(i, n_tiles - 1)


def _finish_tile(i):
    return jnp.maximum(i - 1, 0)


def _rows(tm, width, tile_of_step):
    return pl.BlockSpec((tm, width), lambda i: (tile_of_step(i), 0))


def _cast_plan(weights, n_tiles):
    specs, shapes = [], []
    tile = _matmul_tile(n_tiles)
    for w in weights:
        rows, cols = w.shape
        n_blocks = n_tiles
        while (rows // n_blocks) % BF16_ROWS or rows % n_blocks:
            n_blocks //= 2
        repeat = n_tiles // n_blocks
        specs.append(pl.BlockSpec((rows // n_blocks, cols),
                                  lambda i, repeat=repeat: (tile(i) // repeat, 0)))
        shapes.append(jax.ShapeDtypeStruct(w.shape, BF16))
    return specs, shapes


def _ffn_ln(x, win, wo, g, b, next_weights, alpha):
    n_tiles = ROWS // TM_FFN
    cast_specs, cast_shapes = _cast_plan(next_weights, n_tiles)
    return pl.pallas_call(
        functools.partial(_ffn_ln_kernel, alpha, len(next_weights), n_tiles),
        grid=(n_tiles + 1,),
        in_specs=[_rows(TM_FFN, D_MODEL, _matmul_tile(n_tiles)), _resident(win.shape),
                  _resident(wo.shape), _resident(g.shape), _resident(b.shape)] + cast_specs,
        out_specs=[_rows(TM_FFN, D_MODEL, _finish_tile)] + cast_specs,
        out_shape=[jax.ShapeDtypeStruct((ROWS, D_MODEL), F32)] + cast_shapes,
        scratch_shapes=[pltpu.VMEM((TM_FFN, D_MODEL), F32)],
        compiler_params=pltpu.CompilerParams(
            dimension_semantics=("arbitrary",), vmem_limit_bytes=VMEM_LIMIT_BYTES),
        name="ffn_ln",
    )(x, win, wo, g, b, *next_weights)


def _ffn_ple(x, p, win, wo, g3, b3, wpi, wpg, g4, b4, alpha):
    n_tiles = ROWS // TM_FFN
    return pl.pallas_call(
        functools.partial(_ffn_ple_kernel, alpha, n_tiles),
        grid=(n_tiles + 1,),
        in_specs=[_rows(TM_FFN, D_MODEL, _matmul_tile(n_tiles)),
                  _rows(TM_FFN, PLE_DIM, _finish_tile), _resident(win.shape),
                  _resident(wo.shape), _resident(g3.shape),
                  _resident(b3.shape), _resident(wpi.shape), _resident(wpg.shape),
                  _resident(g4.shape), _resident(b4.shape)],
        out_specs=_rows(TM_FFN, D_MODEL, _finish_tile),
        out_shape=jax.ShapeDtypeStruct((ROWS, D_MODEL), F32),
        scratch_shapes=[pltpu.VMEM((TM_FFN, D_MODEL), F32)],
        compiler_params=pltpu.CompilerParams(
            dimension_semantics=("arbitrary",), vmem_limit_bytes=VMEM_LIMIT_BYTES),
        name="ffn_ple",
    )(x, p, win, wo, g3, b3, wpi, wpg, g4, b4)


def _mixer(x, win, cw, cb, cwo, lam, bd, cd, dsk, perm, unperm, wglu, wout, g, b,
           next_weights, alpha):
    consts = (win, cw, cb, cwo, lam, bd, cd, dsk, perm, unperm, wglu, wout, g, b)
    n_tiles = SEQ // TL_MIX
    cast_specs, cast_shapes = _cast_plan(next_weights, n_tiles)
    matmul_tile = _matmul_tile(n_tiles)
    block = (BATCH, TL_MIX, D_MODEL)
    return pl.pallas_call(
        functools.partial(_mixer_kernel, alpha, len(next_weights), n_tiles),
        grid=(n_tiles + 1,),
        in_specs=[pl.BlockSpec(block, lambda i: (0, matmul_tile(i), 0))]
        + [_resident(c.shape) for c in consts] + cast_specs,
        out_specs=[pl.BlockSpec(block, lambda i: (0, _finish_tile(i), 0))] + cast_specs,
        out_shape=[jax.ShapeDtypeStruct((BATCH, SEQ, D_MODEL), F32)] + cast_shapes,
        scratch_shapes=[
            pltpu.VMEM((TM_MIX, 4 * HALF_ST), F32),
            pltpu.VMEM((SUBLANES, 4 * HALF_ST), F32),
            pltpu.VMEM((BATCH, CONV_PAD + TL_MIX, CONV_CH), F32),
            pltpu.VMEM((TM_MIX, D_MODEL), F32),
        ],
        compiler_params=pltpu.CompilerParams(
            dimension_semantics=("arbitrary",), vmem_limit_bytes=VMEM_LIMIT_BYTES),
        name="mixer",
    )(x, *consts, *next_weights)


def _block_diag(blocks):
    n, a, b = blocks.shape
    eye = jnp.eye(n, dtype=blocks.dtype)
    return jnp.einsum("gab,gk->gakb", blocks, eye).reshape(n * a, n * b)


def _ssm_matrices(lam_re, lam_im, log_step, b_re, b_im, c_re, c_im):
    dt = jnp.exp(log_step)[:, None]
    mag = jnp.exp(lam_re * dt)
    lbar_re = mag * jnp.cos(lam_im * dt)
    lbar_im = mag * jnp.sin(lam_im * dt)
    den = lam_re * lam_re + lam_im * lam_im
    q_re = ((lbar_re - 1.0) * lam_re + lbar_im * lam_im) / den
    q_im = (lbar_im * lam_re - (lbar_re - 1.0) * lam_im) / den
    bbar_re = q_re[..., None] * b_re - q_im[..., None] * b_im
    bbar_im = q_re[..., None] * b_im + q_im[..., None] * b_re

    lam_cols, bd, cd = [], [], []
    for h in range(2):
        gs = slice(h * HALF_GROUPS, (h + 1) * HALF_GROUPS)
        lam_cols += [lbar_re[gs].reshape(-1), lbar_im[gs].reshape(-1)]
        bd.append(jnp.concatenate(
            [_block_diag(jnp.swapaxes(bbar_re[gs], 1, 2)),
             _block_diag(jnp.swapaxes(bbar_im[gs], 1, 2))], axis=1))
        cd.append(jnp.concatenate(
            [_block_diag(jnp.swapaxes(c_re[gs], 1, 2)),
             _block_diag(jnp.swapaxes(-c_im[gs], 1, 2))], axis=0))
    lam = jnp.broadcast_to(jnp.concatenate(lam_cols)[None, :], (SUBLANES, 4 * HALF_ST))
    return lam, jnp.stack(bd).astype(BF16), jnp.stack(cd).astype(BF16)


def _row_permutations():
    dst = np.arange(PERM_ROWS)
    src = (dst % BATCH) * PERM_TL + dst // BATCH
    perm = (src[:, None] == np.arange(PERM_ROWS)[None, :]).astype(np.float32)
    return (jnp.asarray(perm, dtype=BF16),
            jnp.asarray(np.concatenate([perm.T, perm.T], axis=1), dtype=BF16))


def kernel(x, p, ffn1_w_in, ffn1_w_out, ln1_g, ln1_b, mix_w_in, conv_w, conv_b, conv_w_out, ssm_lam_re, ssm_lam_im, ssm_log_step, ssm_b_re, ssm_b_im, ssm_c_re, ssm_c_im, ssm_d, ssm_w_glu, mix_w_out, ln2_g, ln2_b, ffn2_w_in, ffn2_w_out, ln3_g, ln3_b, ple_w_in, ple_w_gate, ln4_g, ln4_b):
    depth = p.shape[0]
    alpha = (2.0 * depth) ** 0.25
    assert x.shape == (BATCH, SEQ, D_MODEL) and BATCH == SUBLANES

    def row(v, i):
        return v[i:i + 1].reshape(1, -1)

    perm, unperm = _row_permutations()
    h = x.reshape(ROWS, D_MODEL)
    for i in range(depth):
        h, mw_in, cw_out, w_glu, mw_out = _ffn_ln(
            h, ffn1_w_in[i].astype(BF16), ffn1_w_out[i].astype(BF16), row(ln1_g, i),
            row(ln1_b, i), (mix_w_in[i], conv_w_out[i], ssm_w_glu[i], mix_w_out[i]), alpha)
        lam, bd, cd = _ssm_matrices(ssm_lam_re[i], ssm_lam_im[i], ssm_log_step[i],
                                    ssm_b_re[i], ssm_b_im[i], ssm_c_re[i], ssm_c_im[i])
        h, f2_in, f2_out, pw_in, pw_gate = _mixer(
            h.reshape(BATCH, SEQ, D_MODEL), mw_in, conv_w[i], row(conv_b, i), cw_out, lam,
            bd, cd, row(ssm_d, i), perm, unperm, w_glu, mw_out, row(ln2_g, i), row(ln2_b, i),
            (ffn2_w_in[i], ffn2_w_out[i], ple_w_in[i], ple_w_gate[i]), alpha)
        h = _ffn_ple(h.reshape(ROWS, D_MODEL), p[i].reshape(ROWS, PLE_DIM), f2_in, f2_out,
                     row(ln3_g, i), row(ln3_b, i), pw_in, pw_gate,
                     row(ln4_g, i), row(ln4_b, i), alpha)
    return h.reshape(BATCH, SEQ, D_MODEL)
```

```python
import functools

import jax
import jax.numpy as jnp
import numpy as np
from jax import lax
from jax.experimental import pallas as pl
from jax.experimental.pallas import tpu as pltpu

D_MODEL = 1024
BATCH = 8
SEQ = 2048
PLE_DIM = 256
D_FF = 2816
CONV_CH = 512
CONV_K = 3
SSM_WIDTH = 512
SSM_GROUP = 16
SSM_GROUPS = SSM_WIDTH // SSM_GROUP
SSM_STATE = 64
LN_EPS = 1e-5

SUBLANES = 8
LANES = 128
BF16_ROWS = 16
VMEM_LIMIT_BYTES = 56 * 1024 * 1024

ROWS = BATCH * SEQ
TM_FFN = 512
TL_MIX = 64
TM_MIX = BATCH * TL_MIX
FF_CHUNKS = ((0, 768), (768, 768), (1536, 768), (2304, 512))
FF_ANCHORS = 2
W_CAST_STEPS = 8
HALF_GROUPS = SSM_GROUPS // 2
HALF_IN = HALF_GROUPS * SSM_GROUP
HALF_ST = HALF_GROUPS * SSM_STATE
CONV_PAD = SUBLANES
MXU_DIM = 256
PERM_ROWS = MXU_DIM
PERM_TL = PERM_ROWS // BATCH

BF16 = jnp.bfloat16
F32 = jnp.float32


def _dot(a, b):
    return jnp.dot(a, b, preferred_element_type=F32)


def _layer_norm(y, g, b):
    mu = jnp.mean(y, axis=-1, keepdims=True)
    yc = y - mu
    var = jnp.mean(yc * yc, axis=-1, keepdims=True)
    return yc * lax.rsqrt(var + LN_EPS) * g + b


def _after(a, done):
    assert a.shape[0] == done.shape[0]
    dep = done[:, 0:LANES]
    for c in range(LANES, done.shape[1], LANES):
        dep = jnp.maximum(dep, done[:, c:c + LANES])
    dep = jnp.concatenate([dep] * (a.shape[1] // LANES), axis=1)
    return jnp.where(pl.program_id(0) >= 0, a, dep)


def _swiglu(xb, win_ref, wo_ref, finish):
    rows = xb.shape[0] // FF_ANCHORS
    acc = None
    pending = None
    for k, (c0, cw) in enumerate(FF_CHUNKS):
        gate = _dot(xb, win_ref[:, c0:c0 + cw])
        up = _dot(xb, win_ref[:, D_FF + c0:D_FF + c0 + cw])
        if k == 0:
            done = finish()
        elif k <= FF_ANCHORS:
            up = jnp.concatenate(
                [_after(up[r0:r0 + rows], done[r0:r0 + rows]) if r0 == (k - 1) * rows
                 else up[r0:r0 + rows] for r0 in range(0, xb.shape[0], rows)], axis=0)
        if pending is not None:
            h, rows_out = pending
            part = _dot(h, wo_ref[rows_out, :])
            acc = part if acc is None else acc + part
        pending = ((gate * jax.nn.sigmoid(gate) * up).astype(BF16), slice(c0, c0 + cw))
    h, rows_out = pending
    return acc + _dot(h, wo_ref[rows_out, :])


def _time_blocks(a, tl):
    return [jnp.concatenate([a[b * tl + t0:b * tl + t0 + PERM_TL] for b in range(BATCH)], axis=0)
            for t0 in range(0, tl, PERM_TL)]


def _cast_blocks(src_refs, dst_refs):
    for src, dst in zip(src_refs, dst_refs, strict=True):
        dst[...] = src[...].astype(BF16)


def _two_stage(n_tiles, pre_ref, matmul_stage, finish_stage, first_step=0):
    step = pl.program_id(0) - first_step

    @pl.when(step == 0)
    def _():
        pre_ref[...] = jnp.zeros_like(pre_ref)

    @pl.when((step >= 0) & (step < n_tiles))
    def _():
        pre_ref[...] = matmul_stage(finish_stage)

    @pl.when(step == n_tiles)
    def _():
        finish_stage()


def _ffn_ln_kernel(alpha, n_cast, n_tiles, x_ref, win_f32_ref, wo_f32_ref, g_ref, b_ref, *rest):
    cast_src, (o_ref, *cast_dst) = rest[:n_cast], rest[n_cast:-3]
    win_ref, wo_ref, pre_ref = rest[-3:]

    step = pl.program_id(0)

    @pl.when(step < W_CAST_STEPS)
    def _():
        for src, dst in ((win_f32_ref, win_ref), (wo_f32_ref, wo_ref)):
            rows = src.shape[0]
            dst[pl.ds(pl.multiple_of(step * rows, rows), rows), :] = src[...].astype(BF16)

    def matmul_stage(finish):
        _cast_blocks(cast_src, cast_dst)
        f = _swiglu(x_ref[...].astype(BF16), win_ref, wo_ref, finish)
        return alpha * x_ref[...] + 0.5 * f

    def finish_stage():
        out = _layer_norm(pre_ref[...], g_ref[...], b_ref[...])
        o_ref[...] = out
        return out

    _two_stage(n_tiles, pre_ref, matmul_stage, finish_stage, first_step=W_CAST_STEPS)


def _ffn_ple_kernel(alpha, n_tiles, x_ref, p_ref, win_ref, wo_ref, g3_ref, b3_ref,
                    wpi_ref, wpg_ref, g4_ref, b4_ref, o_ref, pre_ref):
    def matmul_stage(finish):
        f = _swiglu(x_ref[...].astype(BF16), win_ref, wo_ref, finish)
        return alpha * x_ref[...] + 0.5 * f

    def finish_stage():
        x3 = _layer_norm(pre_ref[...], g3_ref[...], b3_ref[...])
        e = _dot(p_ref[...].astype(BF16), wpi_ref[...]) * jax.nn.sigmoid(
            _dot(x3.astype(BF16), wpg_ref[...]))
        out = _layer_norm(alpha * x3 + e, g4_ref[...], b4_ref[...])
        o_ref[...] = out
        return out

    _two_stage(n_tiles, pre_ref, matmul_stage, finish_stage)


def _mixer_kernel(alpha, n_cast, n_tiles, x_ref, win_ref, cw_ref, cb_ref, cwo_ref, lam_ref,
                  bd_ref, cd_ref, dsk_ref, perm_ref, unperm_ref, wglu_ref, wout_ref, g_ref,
                  b_ref, *rest):
    cast_src, o_ref = rest[:n_cast], rest[n_cast]
    cast_dst = rest[n_cast + 1:2 * n_cast + 1]
    s_ref, st_ref, v_ref, pre_ref = rest[2 * n_cast + 1:]
    tl = x_ref.shape[1]
    tm = BATCH * tl
    step = pl.program_id(0)

    @pl.when(step == 0)
    def _():
        st_ref[...] = jnp.zeros_like(st_ref)
        v_ref[:, 0:CONV_PAD, :] = jnp.zeros((BATCH, CONV_PAD, CONV_CH), F32)

    def finish_stage():
        out = _layer_norm(pre_ref[...], g_ref[...], b_ref[...])
        o_ref[...] = out.reshape(BATCH, tl, D_MODEL)
        return out

    def matmul_stage(finish):
        _cast_blocks(cast_src, cast_dst)
        return _mixer_matmul_stage(
            alpha, tl, x_ref, win_ref, cw_ref, cb_ref, cwo_ref, lam_ref, bd_ref, cd_ref,
            dsk_ref, perm_ref, unperm_ref, wglu_ref, wout_ref, s_ref, st_ref, v_ref, finish)

    _two_stage(n_tiles, pre_ref, matmul_stage, finish_stage)


def _mixer_matmul_stage(alpha, tl, x_ref, win_ref, cw_ref, cb_ref, cwo_ref, lam_ref, bd_ref,
                        cd_ref, dsk_ref, perm_ref, unperm_ref, wglu_ref, wout_ref, s_ref,
                        st_ref, v_ref, finish):
    tm = BATCH * tl
    xb = x_ref[...].reshape(tm, D_MODEL).astype(BF16)

    def proj(c0, cw):
        return _dot(xb, win_ref[:, c0:c0 + cw])

    n_blocks = tl // PERM_TL
    cols = [(slice(2 * h * HALF_ST, (2 * h + 1) * HALF_ST),
             slice((2 * h + 1) * HALF_ST, (2 * h + 2) * HALF_ST)) for h in range(2)]
    block_rows = [slice(j * PERM_ROWS, (j + 1) * PERM_ROWS) for j in range(n_blocks)]

    u_blocks = _time_blocks(proj(3 * CONV_CH, SSM_WIDTH), tl)
    conv_c = proj(CONV_CH, CONV_CH)
    ub_blocks = [_dot(perm_ref[...], u.astype(BF16)).astype(BF16) for u in u_blocks]

    v = conv_c * proj(2 * CONV_CH, CONV_CH)
    v_ref[:, CONV_PAD:CONV_PAD + tl, :] = v.reshape(BATCH, tl, CONV_CH)
    z = cb_ref[...].reshape(1, 1, CONV_CH)
    for k in range(CONV_K):
        lag = CONV_K - 1 - k
        z = z + cw_ref[k:k + 1, :].reshape(1, 1, CONV_CH) * v_ref[
            :, CONV_PAD - lag:CONV_PAD - lag + tl, :]
    v_ref[:, 0:CONV_PAD, :] = v_ref[:, tl:tl + CONV_PAD, :]

    for rows, ub in zip(block_rows, ub_blocks, strict=True):
        for h in range(2):
            s_ref[rows, 2 * h * HALF_ST:2 * (h + 1) * HALF_ST] = _dot(
                ub[:, h * HALF_IN:(h + 1) * HALF_IN], bd_ref[h])
    conv_gated = (proj(0, CONV_CH) * z.reshape(tm, CONV_CH)).astype(BF16)

    def scan_block(j, state):
        out = []
        for (re, im), (sr, si) in zip(cols, state, strict=True):
            lam_re, lam_im = lam_ref[:, re], lam_ref[:, im]
            for l in range(j * PERM_TL, (j + 1) * PERM_TL):
                rows = slice(l * SUBLANES, (l + 1) * SUBLANES)
                sr, si = (lam_re * sr - lam_im * si + s_ref[rows, re],
                          lam_re * si + lam_im * sr + s_ref[rows, im])
                s_ref[rows, re] = sr
                s_ref[rows, im] = si
            out.append((sr, si))
        return out

    def c_proj(j):
        return jnp.concatenate(
            [_dot(s_ref[block_rows[j], 2 * h * HALF_ST:2 * (h + 1) * HALF_ST].astype(BF16),
                  cd_ref[h]) for h in range(2)], axis=-1)

    fillers = [lambda: _dot(conv_gated, cwo_ref[...]),
               lambda: proj(4 * CONV_CH, D_MODEL),
               lambda: proj(4 * CONV_CH + D_MODEL, D_MODEL)]
    filled = []
    state = [(st_ref[:, re], st_ref[:, im]) for re, im in cols]
    y_tb_blocks = []
    for j in range(n_blocks):
        state = scan_block(j, state)
        take = -(-len(fillers) // (n_blocks - j))
        filled += [f() for f in fillers[:take]]
        fillers = fillers[take:]
        y_tb_blocks.append(c_proj(j))
    y_conv, gate_conv, gate_ssm = filled
    for (re, im), (sr, si) in zip(cols, state, strict=True):
        st_ref[:, re] = sr
        st_ref[:, im] = si

    y_blocks = []
    for y_tb, u in zip(y_tb_blocks, u_blocks, strict=True):
        hi = y_tb.astype(BF16)
        lo = (y_tb - hi.astype(F32)).astype(BF16)
        y_blocks.append(
            _dot(unperm_ref[...], jnp.concatenate([hi, lo], axis=0)) + dsk_ref[...] * u)
    merged_blocks = _time_blocks(jax.nn.sigmoid(gate_conv) * y_conv, tl)
    gate_blocks = _time_blocks(gate_ssm, tl)
    sg_blocks = [jax.nn.gelu(y).astype(BF16) for y in y_blocks]
    glu_a = [_dot(sg, wglu_ref[:, 0:D_MODEL]) for sg in sg_blocks]
    glu_b = [_dot(sg, wglu_ref[:, D_MODEL:2 * D_MODEL]) for sg in sg_blocks]
    done_blocks = _time_blocks(finish(), tl)
    pre_blocks = []
    for j in range(n_blocks):
        m = merged_blocks[j] + jax.nn.sigmoid(gate_blocks[j]) * (
            glu_a[j] * jax.nn.sigmoid(glu_b[j]))
        m = _after(m, done_blocks[j])
        x = x_ref[:, j * PERM_TL:(j + 1) * PERM_TL, :].reshape(PERM_ROWS, D_MODEL)
        pre_blocks.append(alpha * x + _dot(m.astype(BF16), wout_ref[...]))
    return jnp.concatenate([blk[b * PERM_TL:(b + 1) * PERM_TL]
                            for b in range(BATCH) for blk in pre_blocks], axis=0)


def _resident(shape):
    zeros = (0,) * len(shape)
    return pl.BlockSpec(shape, lambda i: zeros, pipeline_mode=pl.Buffered(1))


def _matmul_tile(n_tiles, first_step=0):
    return lambda i: jnp.clip(i - first_step, 0, n_tiles - 1)


def _finish_tile(first_step=0):
    return lambda i: jnp.maximum(i - first_step - 1, 0)


def _rows(tm, width, tile_of_step):
    return pl.BlockSpec((tm, width), lambda i: (tile_of_step(i), 0))


def _cast_plan(weights, n_tiles, first_step=0):
    specs, shapes = [], []
    tile = _matmul_tile(n_tiles, first_step)
    for w in weights:
        rows, cols = w.shape
        n_blocks = n_tiles
        while (rows // n_blocks) % BF16_ROWS or rows % n_blocks:
            n_blocks //= 2
        repeat = n_tiles // n_blocks
        specs.append(pl.BlockSpec((rows // n_blocks, cols),
                                  lambda i, repeat=repeat: (tile(i) // repeat, 0)))
        shapes.append(jax.ShapeDtypeStruct(w.shape, BF16))
    return specs, shapes


def _ffn_ln(x, win, wo, g, b, next_weights, alpha):
    n_tiles = ROWS // TM_FFN
    first = W_CAST_STEPS
    cast_specs, cast_shapes = _cast_plan(next_weights, n_tiles, first)

    def own_weight(w):
        return pl.BlockSpec((w.shape[0] // first, w.shape[1]),
                            lambda i: (jnp.minimum(i, first - 1), 0))

    return pl.pallas_call(
        functools.partial(_ffn_ln_kernel, alpha, len(next_weights), n_tiles),
        grid=(first + n_tiles + 1,),
        in_specs=[_rows(TM_FFN, D_MODEL, _matmul_tile(n_tiles, first)), own_weight(win),
                  own_weight(wo), _resident(g.shape), _resident(b.shape)] + cast_specs,
        out_specs=[_rows(TM_FFN, D_MODEL, _finish_tile(first))] + cast_specs,
        out_shape=[jax.ShapeDtypeStruct((ROWS, D_MODEL), F32)] + cast_shapes,
        scratch_shapes=[pltpu.VMEM(win.shape, BF16), pltpu.VMEM(wo.shape, BF16),
                        pltpu.VMEM((TM_FFN, D_MODEL), F32)],
        compiler_params=pltpu.CompilerParams(
            dimension_semantics=("arbitrary",), vmem_limit_bytes=VMEM_LIMIT_BYTES),
        name="ffn_ln",
    )(x, win, wo, g, b, *next_weights)


def _ffn_ple(x, p, win, wo, g3, b3, wpi, wpg, g4, b4, alpha):
    n_tiles = ROWS // TM_FFN
    return pl.pallas_call(
        functools.partial(_ffn_ple_kernel, alpha, n_tiles),
        grid=(n_tiles + 1,),
        in_specs=[_rows(TM_FFN, D_MODEL, _matmul_tile(n_tiles)),
                  _rows(TM_FFN, PLE_DIM, _finish_tile()), _resident(win.shape),
                  _resident(wo.shape), _resident(g3.shape),
                  _resident(b3.shape), _resident(wpi.shape), _resident(wpg.shape),
                  _resident(g4.shape), _resident(b4.shape)],
        out_specs=_rows(TM_FFN, D_MODEL, _finish_tile()),
        out_shape=jax.ShapeDtypeStruct((ROWS, D_MODEL), F32),
        scratch_shapes=[pltpu.VMEM((TM_FFN, D_MODEL), F32)],
        compiler_params=pltpu.CompilerParams(
            dimension_semantics=("arbitrary",), vmem_limit_bytes=VMEM_LIMIT_BYTES),
        name="ffn_ple",
    )(x, p, win, wo, g3, b3, wpi, wpg, g4, b4)


def _mixer(x, win, cw, cb, cwo, lam, bd, cd, dsk, perm, unperm, wglu, wout, g, b,
           next_weights, alpha):
    consts = (win, cw, cb, cwo, lam, bd, cd, dsk, perm, unperm, wglu, wout, g, b)
    n_tiles = SEQ // TL_MIX
    cast_specs, cast_shapes = _cast_plan(next_weights, n_tiles)
    matmul_tile, finish_tile = _matmul_tile(n_tiles), _finish_tile()
    block = (BATCH, TL_MIX, D_MODEL)
    return pl.pallas_call(
        functools.partial(_mixer_kernel, alpha, len(next_weights), n_tiles),
        grid=(n_tiles + 1,),
        in_specs=[pl.BlockSpec(block, lambda i: (0, matmul_tile(i), 0))]
        + [_resident(c.shape) for c in consts] + cast_specs,
        out_specs=[pl.BlockSpec(block, lambda i: (0, finish_tile(i), 0))] + cast_specs,
        out_shape=[jax.ShapeDtypeStruct((BATCH, SEQ, D_MODEL), F32)] + cast_shapes,
        scratch_shapes=[
            pltpu.VMEM((TM_MIX, 4 * HALF_ST), F32),
            pltpu.VMEM((SUBLANES, 4 * HALF_ST), F32),
            pltpu.VMEM((BATCH, CONV_PAD + TL_MIX, CONV_CH), F32),
            pltpu.VMEM((TM_MIX, D_MODEL), F32),
        ],
        compiler_params=pltpu.CompilerParams(
            dimension_semantics=("arbitrary",), vmem_limit_bytes=VMEM_LIMIT_BYTES),
        name="mixer",
    )(x, *consts, *next_weights)


def _block_diag(blocks):
    n, a, b = blocks.shape
    eye = jnp.eye(n, dtype=blocks.dtype)
    return jnp.einsum("gab,gk->gakb", blocks, eye).reshape(n * a, n * b)


def _ssm_matrices(lam_re, lam_im, log_step, b_re, b_im, c_re, c_im):
    dt = jnp.exp(log_step)[:, None]
    mag = jnp.exp(lam_re * dt)
    lbar_re = mag * jnp.cos(lam_im * dt)
    lbar_im = mag * jnp.sin(lam_im * dt)
    den = lam_re * lam_re + lam_im * lam_im
    q_re = ((lbar_re - 1.0) * lam_re + lbar_im * lam_im) / den
    q_im = (lbar_im * lam_re - (lbar_re - 1.0) * lam_im) / den
    bbar_re = q_re[..., None] * b_re - q_im[..., None] * b_im
    bbar_im = q_re[..., None] * b_im + q_im[..., None] * b_re

    lam_cols, bd, cd = [], [], []
    for h in range(2):
        gs = slice(h * HALF_GROUPS, (h + 1) * HALF_GROUPS)
        lam_cols += [lbar_re[gs].reshape(-1), lbar_im[gs].reshape(-1)]
        bd.append(jnp.concatenate(
            [_block_diag(jnp.swapaxes(bbar_re[gs], 1, 2)),
             _block_diag(jnp.swapaxes(bbar_im[gs], 1, 2))], axis=1))
        cd.append(jnp.concatenate(
            [_block_diag(jnp.swapaxes(c_re[gs], 1, 2)),
             _block_diag(jnp.swapaxes(-c_im[gs], 1, 2))], axis=0))
    lam = jnp.broadcast_to(jnp.concatenate(lam_cols)[None, :], (SUBLANES, 4 * HALF_ST))
    return lam, jnp.stack(bd).astype(BF16), jnp.stack(cd).astype(BF16)


def _row_permutations():
    dst = np.arange(PERM_ROWS)
    src = (dst % BATCH) * PERM_TL + dst // BATCH
    perm = (src[:, None] == np.arange(PERM_ROWS)[None, :]).astype(np.float32)
    return (jnp.asarray(perm, dtype=BF16),
            jnp.asarray(np.concatenate([perm.T, perm.T], axis=1), dtype=BF16))


def kernel(x, p, ffn1_w_in, ffn1_w_out, ln1_g, ln1_b, mix_w_in, conv_w, conv_b, conv_w_out, ssm_lam_re, ssm_lam_im, ssm_log_step, ssm_b_re, ssm_b_im, ssm_c_re, ssm_c_im, ssm_d, ssm_w_glu, mix_w_out, ln2_g, ln2_b, ffn2_w_in, ffn2_w_out, ln3_g, ln3_b, ple_w_in, ple_w_gate, ln4_g, ln4_b):
    depth = p.shape[0]
    alpha = (2.0 * depth) ** 0.25
    assert x.shape == (BATCH, SEQ, D_MODEL) and BATCH == SUBLANES

    def row(v, i):
        return v[i:i + 1].reshape(1, -1)

    perm, unperm = _row_permutations()
    h = x.reshape(ROWS, D_MODEL)
    for i in range(depth):
        h, mw_in, cw_out, w_glu, mw_out = _ffn_ln(
            h, ffn1_w_in[i], ffn1_w_out[i], row(ln1_g, i),
            row(ln1_b, i), (mix_w_in[i], conv_w_out[i], ssm_w_glu[i], mix_w_out[i]), alpha)
        lam, bd, cd = _ssm_matrices(ssm_lam_re[i], ssm_lam_im[i], ssm_log_step[i],
                                    ssm_b_re[i], ssm_b_im[i], ssm_c_re[i], ssm_c_im[i])
        h, f2_in, f2_out, pw_in, pw_gate = _mixer(
            h.reshape(BATCH, SEQ, D_MODEL), mw_in, conv_w[i], row(conv_b, i), cw_out, lam,
            bd, cd, row(ssm_d, i), perm, unperm, w_glu, mw_out, row(ln2_g, i), row(ln2_b, i),
            (ffn2_w_in[i], ffn2_w_out[i], ple_w_in[i], ple_w_gate[i]), alpha)
        h = _ffn_ple(h.reshape(ROWS, D_MODEL), p[i].reshape(ROWS, PLE_DIM), f2_in, f2_out,
                     row(ln3_g, i), row(ln3_b, i), pw_in, pw_gate,
                     row(ln4_g, i), row(ln4_b, i), alpha)
    return h.reshape(BATCH, SEQ, D_MODEL)
```

```python
import functools

import jax
import jax.numpy as jnp
import numpy as np
from jax import lax
from jax.experimental import pallas as pl
from jax.experimental.pallas import tpu as pltpu

D_MODEL = 1024
BATCH = 8
SEQ = 2048
PLE_DIM = 256
D_FF = 2816
CONV_CH = 512
CONV_K = 3
SSM_WIDTH = 512
SSM_GROUP = 16
SSM_GROUPS = SSM_WIDTH // SSM_GROUP
SSM_STATE = 64
LN_EPS = 1e-5

SUBLANES = 8
LANES = 128
BF16_ROWS = 16
VMEM_LIMIT_BYTES = 56 * 1024 * 1024

ROWS = BATCH * SEQ
TM_FFN = 512
TL_MIX = 64
TM_MIX = BATCH * TL_MIX
FF_CHUNKS = ((0, 768), (768, 768), (1536, 768), (2304, 512))
FF_ANCHORS = 2
W_CAST_STEPS = 8
HALF_GROUPS = SSM_GROUPS // 2
HALF_IN = HALF_GROUPS * SSM_GROUP
HALF_ST = HALF_GROUPS * SSM_STATE
CONV_PAD = SUBLANES
MXU_DIM = 256
PERM_ROWS = MXU_DIM
PERM_TL = PERM_ROWS // BATCH

BF16 = jnp.bfloat16
F32 = jnp.float32


def _dot(a, b):
    return jnp.dot(a, b, preferred_element_type=F32)


def _layer_norm(y, g, b):
    mu = jnp.mean(y, axis=-1, keepdims=True)
    yc = y - mu
    var = jnp.mean(yc * yc, axis=-1, keepdims=True)
    return yc * lax.rsqrt(var + LN_EPS) * g + b


def _after(a, done):
    assert a.shape[0] == done.shape[0]
    dep = done[:, 0:LANES]
    for c in range(LANES, done.shape[1], LANES):
        dep = jnp.maximum(dep, done[:, c:c + LANES])
    dep = jnp.concatenate([dep] * (a.shape[1] // LANES), axis=1)
    return jnp.where(pl.program_id(0) >= 0, a, dep)


def _swiglu(xb, win_ref, wo_ref, finish):
    rows = xb.shape[0] // FF_ANCHORS
    acc = None
    pending = None
    for k, (c0, cw) in enumerate(FF_CHUNKS):
        gate = _dot(xb, win_ref[:, c0:c0 + cw])
        up = _dot(xb, win_ref[:, D_FF + c0:D_FF + c0 + cw])
        if k == 0:
            done = finish()
        elif k <= FF_ANCHORS:
            up = jnp.concatenate(
                [_after(up[r0:r0 + rows], done[r0:r0 + rows]) if r0 == (k - 1) * rows
                 else up[r0:r0 + rows] for r0 in range(0, xb.shape[0], rows)], axis=0)
        if pending is not None:
            h, rows_out = pending
            part = _dot(h, wo_ref[rows_out, :])
            acc = part if acc is None else acc + part
        pending = ((gate * jax.nn.sigmoid(gate) * up).astype(BF16), slice(c0, c0 + cw))
    h, rows_out = pending
    return acc + _dot(h, wo_ref[rows_out, :])


def _time_blocks(a, tl):
    return [jnp.concatenate([a[b * tl + t0:b * tl + t0 + PERM_TL] for b in range(BATCH)], axis=0)
            for t0 in range(0, tl, PERM_TL)]


def _cast_blocks(src_refs, dst_refs):
    for src, dst in zip(src_refs, dst_refs, strict=True):
        dst[...] = src[...].astype(BF16)


def _two_stage(n_tiles, pre_ref, matmul_stage, finish_stage, first_step=0):
    step = pl.program_id(0) - first_step

    @pl.when(step == 0)
    def _():
        pre_ref[...] = jnp.zeros_like(pre_ref)

    @pl.when((step >= 0) & (step < n_tiles))
    def _():
        pre_ref[...] = matmul_stage(finish_stage)

    @pl.when(step == n_tiles)
    def _():
        finish_stage()


def _ffn_ln_kernel(alpha, n_cast, n_tiles, x_ref, win_f32_ref, wo_f32_ref, g_ref, b_ref, *rest):
    cast_src, (o_ref, *cast_dst) = rest[:n_cast], rest[n_cast:-3]
    win_ref, wo_ref, pre_ref = rest[-3:]

    step = pl.program_id(0)

    @pl.when(step < W_CAST_STEPS)
    def _():
        for src, dst in ((win_f32_ref, win_ref), (wo_f32_ref, wo_ref)):
            rows = src.shape[0]
            dst[pl.ds(pl.multiple_of(step * rows, rows), rows), :] = src[...].astype(BF16)

    def matmul_stage(finish):
        _cast_blocks(cast_src, cast_dst)
        f = _swiglu(x_ref[...].astype(BF16), win_ref, wo_ref, finish)
        return alpha * x_ref[...] + 0.5 * f

    def finish_stage():
        out = _layer_norm(pre_ref[...], g_ref[...], b_ref[...])
        o_ref[...] = out
        return out

    _two_stage(n_tiles, pre_ref, matmul_stage, finish_stage, first_step=W_CAST_STEPS)


def _ffn_ple_kernel(alpha, n_tiles, x_ref, p_ref, win_ref, wo_ref, g3_ref, b3_ref,
                    wpi_ref, wpg_ref, g4_ref, b4_ref, o_ref, pre_ref):
    def matmul_stage(finish):
        f = _swiglu(x_ref[...].astype(BF16), win_ref, wo_ref, finish)
        return alpha * x_ref[...] + 0.5 * f

    def finish_stage():
        x3 = _layer_norm(pre_ref[...], g3_ref[...], b3_ref[...])
        e = _dot(p_ref[...].astype(BF16), wpi_ref[...]) * jax.nn.sigmoid(
            _dot(x3.astype(BF16), wpg_ref[...]))
        out = _layer_norm(alpha * x3 + e, g4_ref[...], b4_ref[...])
        o_ref[...] = out
        return out

    _two_stage(n_tiles, pre_ref, matmul_stage, finish_stage)


def _mixer_kernel(alpha, n_cast, n_tiles, x_ref, win_ref, cw_ref, cb_ref, cwo_ref, lam_ref,
                  bd_ref, cd_ref, dsk_ref, perm_ref, unperm_ref, wglu_ref, wout_ref, g_ref,
                  b_ref, *rest):
    cast_src, o_ref = rest[:n_cast], rest[n_cast]
    cast_dst = rest[n_cast + 1:2 * n_cast + 1]
    s_ref, st_ref, v_ref, pre_ref = rest[2 * n_cast + 1:]
    tl = x_ref.shape[1]
    tm = BATCH * tl
    step = pl.program_id(0)

    @pl.when(step == 0)
    def _():
        st_ref[...] = jnp.zeros_like(st_ref)
        v_ref[:, 0:CONV_PAD, :] = jnp.zeros((BATCH, CONV_PAD, CONV_CH), F32)

    def finish_stage():
        out = _layer_norm(pre_ref[...], g_ref[...], b_ref[...])
        o_ref[...] = out.reshape(BATCH, tl, D_MODEL)
        return out

    def matmul_stage(finish):
        _cast_blocks(cast_src, cast_dst)
        return _mixer_matmul_stage(
            alpha, tl, x_ref, win_ref, cw_ref, cb_ref, cwo_ref, lam_ref, bd_ref, cd_ref,
            dsk_ref, perm_ref, unperm_ref, wglu_ref, wout_ref, s_ref, st_ref, v_ref, finish)

    _two_stage(n_tiles, pre_ref, matmul_stage, finish_stage)


def _mixer_matmul_stage(alpha, tl, x_ref, win_ref, cw_ref, cb_ref, cwo_ref, lam_ref, bd_ref,
                        cd_ref, dsk_ref, perm_ref, unperm_ref, wglu_ref, wout_ref, s_ref,
                        st_ref, v_ref, finish):
    tm = BATCH * tl
    xb = x_ref[...].reshape(tm, D_MODEL).astype(BF16)

    def proj(c0, cw):
        return _dot(xb, win_ref[:, c0:c0 + cw])

    n_blocks = tl // PERM_TL
    cols = [(slice(2 * h * HALF_ST, (2 * h + 1) * HALF_ST),
             slice((2 * h + 1) * HALF_ST, (2 * h + 2) * HALF_ST)) for h in range(2)]
    block_rows = [slice(j * PERM_ROWS, (j + 1) * PERM_ROWS) for j in range(n_blocks)]

    u_blocks = _time_blocks(proj(3 * CONV_CH, SSM_WIDTH), tl)
    conv_c = proj(CONV_CH, CONV_CH)
    ub_blocks = [_dot(perm_ref[...], u.astype(BF16)).astype(BF16) for u in u_blocks]

    v = conv_c * proj(2 * CONV_CH, CONV_CH)
    v_ref[:, CONV_PAD:CONV_PAD + tl, :] = v.reshape(BATCH, tl, CONV_CH)
    z = cb_ref[...].reshape(1, 1, CONV_CH)
    for k in range(CONV_K):
        lag = CONV_K - 1 - k
        z = z + cw_ref[k:k + 1, :].reshape(1, 1, CONV_CH) * v_ref[
            :, CONV_PAD - lag:CONV_PAD - lag + tl, :]
    v_ref[:, 0:CONV_PAD, :] = v_ref[:, tl:tl + CONV_PAD, :]

    for rows, ub in zip(block_rows, ub_blocks, strict=True):
        for h in range(2):
            s_ref[rows, 2 * h * HALF_ST:2 * (h + 1) * HALF_ST] = _dot(
                ub[:, h * HALF_IN:(h + 1) * HALF_IN], bd_ref[h])
    conv_gated = (proj(0, CONV_CH) * z.reshape(tm, CONV_CH)).astype(BF16)

    def scan_block(j, state):
        out = []
        for (re, im), (sr, si) in zip(cols, state, strict=True):
            lam_re, lam_im = lam_ref[:, re], lam_ref[:, im]
            for l in range(j * PERM_TL, (j + 1) * PERM_TL):
                rows = slice(l * SUBLANES, (l + 1) * SUBLANES)
                sr, si = (lam_re * sr - lam_im * si + s_ref[rows, re],
                          lam_re * si + lam_im * sr + s_ref[rows, im])
                s_ref[rows, re] = sr
                s_ref[rows, im] = si
            out.append((sr, si))
        return out

    def c_proj(j):
        return jnp.concatenate(
            [_dot(s_ref[block_rows[j], 2 * h * HALF_ST:2 * (h + 1) * HALF_ST].astype(BF16),
                  cd_ref[h]) for h in range(2)], axis=-1)

    fillers = [lambda: _dot(conv_gated, cwo_ref[...]),
               lambda: proj(4 * CONV_CH, D_MODEL),
               lambda: proj(4 * CONV_CH + D_MODEL, D_MODEL)]
    filled = []
    state = [(st_ref[:, re], st_ref[:, im]) for re, im in cols]
    y_tb_blocks = []
    for j in range(n_blocks):
        state = scan_block(j, state)
        take = -(-len(fillers) // (n_blocks - j))
        filled += [f() for f in fillers[:take]]
        fillers = fillers[take:]
        y_tb_blocks.append(c_proj(j))
    y_conv, gate_conv, gate_ssm = filled
    for (re, im), (sr, si) in zip(cols, state, strict=True):
        st_ref[:, re] = sr
        st_ref[:, im] = si

    y_blocks = []
    for y_tb, u in zip(y_tb_blocks, u_blocks, strict=True):
        hi = y_tb.astype(BF16)
        lo = (y_tb - hi.astype(F32)).astype(BF16)
        y_blocks.append(
            _dot(unperm_ref[...], jnp.concatenate([hi, lo], axis=0)) + dsk_ref[...] * u)
    merged_blocks = _time_blocks(jax.nn.sigmoid(gate_conv) * y_conv, tl)
    gate_blocks = _time_blocks(gate_ssm, tl)
    sg_blocks = [jax.nn.gelu(y).astype(BF16) for y in y_blocks]
    glu_a = [_dot(sg, wglu_ref[:, 0:D_MODEL]) for sg in sg_blocks]
    glu_b = [_dot(sg, wglu_ref[:, D_MODEL:2 * D_MODEL]) for sg in sg_blocks]
    done_blocks = _time_blocks(finish(), tl)
    pre_blocks = []
    for j in range(n_blocks):
        m = merged_blocks[j] + jax.nn.sigmoid(gate_blocks[j]) * (
            glu_a[j] * jax.nn.sigmoid(glu_b[j]))
        m = _after(m, done_blocks[j])
        x = x_ref[:, j * PERM_TL:(j + 1) * PERM_TL, :].reshape(PERM_ROWS, D_MODEL)
        pre_blocks.append(alpha * x + _dot(m.astype(BF16), wout_ref[...]))
    return jnp.concatenate([blk[b * PERM_TL:(b + 1) * PERM_TL]
                            for b in range(BATCH) for blk in pre_blocks], axis=0)


def _resident(shape):
    zeros = (0,) * len(shape)
    return pl.BlockSpec(shape, lambda i: zeros, pipeline_mode=pl.Buffered(1))


def _matmul_tile(n_tiles, first_step=0):
    return lambda i: jnp.clip(i - first_step, 0, n_tiles - 1)


def _finish_tile(first_step=0):
    return lambda i: jnp.maximum(i - first_step - 1, 0)


def _rows(tm, width, tile_of_step):
    return pl.BlockSpec((tm, width), lambda i: (tile_of_step(i), 0))


def _cast_plan(weights, n_tiles, first_step=0):
    specs, shapes = [], []
    tile = _matmul_tile(n_tiles, first_step)
    for w in weights:
        rows, cols = w.shape
        n_blocks = n_tiles
        while (rows // n_blocks) % BF16_ROWS or rows % n_blocks:
            n_blocks //= 2
        repeat = n_tiles // n_blocks
        specs.append(pl.BlockSpec((rows // n_blocks, cols),
                                  lambda i, repeat=repeat: (tile(i) // repeat, 0)))
        shapes.append(jax.ShapeDtypeStruct(w.shape, BF16))
    return specs, shapes


def _ffn_ln(x, win, wo, g, b, next_weights, alpha):
    n_tiles = ROWS // TM_FFN
    first = W_CAST_STEPS
    cast_specs, cast_shapes = _cast_plan(next_weights, n_tiles, first)

    def own_weight(w):
        return pl.BlockSpec((w.shape[0] // first, w.shape[1]),
                            lambda i: (jnp.minimum(i, first - 1), 0))

    return pl.pallas_call(
        functools.partial(_ffn_ln_kernel, alpha, len(next_weights), n_tiles),
        grid=(first + n_tiles + 1,),
        in_specs=[_rows(TM_FFN, D_MODEL, _matmul_tile(n_tiles, first)), own_weight(win),
                  own_weight(wo), _resident(g.shape), _resident(b.shape)] + cast_specs,
        out_specs=[_rows(TM_FFN, D_MODEL, _finish_tile(first))] + cast_specs,
        out_shape=[jax.ShapeDtypeStruct((ROWS, D_MODEL), F32)] + cast_shapes,
        scratch_shapes=[pltpu.VMEM(win.shape, BF16), pltpu.VMEM(wo.shape, BF16),
                        pltpu.VMEM((TM_FFN, D_MODEL), F32)],
        compiler_params=pltpu.CompilerParams(
            dimension_semantics=("arbitrary",), vmem_limit_bytes=VMEM_LIMIT_BYTES),
        name="ffn_ln",
    )(x, win, wo, g, b, *next_weights)


def _ffn_ple(x, p, win, wo, g3, b3, wpi, wpg, g4, b4, alpha):
    n_tiles = ROWS // TM_FFN
    return pl.pallas_call(
        functools.partial(_ffn_ple_kernel, alpha, n_tiles),
        grid=(n_tiles + 1,),
        in_specs=[_rows(TM_FFN, D_MODEL, _matmul_tile(n_tiles)),
                  _rows(TM_FFN, PLE_DIM, _finish_tile()), _resident(win.shape),
                  _resident(wo.shape), _resident(g3.shape),
                  _resident(b3.shape), _resident(wpi.shape), _resident(wpg.shape),
                  _resident(g4.shape), _resident(b4.shape)],
        out_specs=_rows(TM_FFN, D_MODEL, _finish_tile()),
        out_shape=jax.ShapeDtypeStruct((ROWS, D_MODEL), F32),
        scratch_shapes=[pltpu.VMEM((TM_FFN, D_MODEL), F32)],
        compiler_params=pltpu.CompilerParams(
            dimension_semantics=("arbitrary",), vmem_limit_bytes=VMEM_LIMIT_BYTES),
        name="ffn_ple",
    )(x, p, win, wo, g3, b3, wpi, wpg, g4, b4)


def _mixer(x, win, cw, cb, cwo, lam, bd, cd, dsk, perm, unperm, wglu, wout, g, b,
           next_weights, alpha):
    consts = (win, cw, cb, cwo, lam, bd, cd, dsk, perm, unperm, wglu, wout, g, b)
    n_tiles = SEQ // TL_MIX
    cast_specs, cast_shapes = _cast_plan(next_weights, n_tiles)
    matmul_tile, finish_tile = _matmul_tile(n_tiles), _finish_tile()
    block = (BATCH, TL_MIX, D_MODEL)
    return pl.pallas_call(
        functools.partial(_mixer_kernel, alpha, len(next_weights), n_tiles),
        grid=(n_tiles + 1,),
        in_specs=[pl.BlockSpec(block, lambda i: (0, matmul_tile(i), 0))]
        + [_resident(c.shape) for c in consts] + cast_specs,
        out_specs=[pl.BlockSpec(block, lambda i: (0, finish_tile(i), 0))] + cast_specs,
        out_shape=[jax.ShapeDtypeStruct((BATCH, SEQ, D_MODEL), F32)] + cast_shapes,
        scratch_shapes=[
            pltpu.VMEM((TM_MIX, 4 * HALF_ST), F32),
            pltpu.VMEM((SUBLANES, 4 * HALF_ST), F32),
            pltpu.VMEM((BATCH, CONV_PAD + TL_MIX, CONV_CH), F32),
            pltpu.VMEM((TM_MIX, D_MODEL), F32),
        ],
        compiler_params=pltpu.CompilerParams(
            dimension_semantics=("arbitrary",), vmem_limit_bytes=VMEM_LIMIT_BYTES),
        name="mixer",
    )(x, *consts, *next_weights)


def _ssm_matrices(lam_re, lam_im, log_step, b_re, b_im, c_re, c_im):
    dt = jnp.exp(log_step)[:, None]
    mag = jnp.exp(lam_re * dt)
    lbar_re = mag * jnp.cos(lam_im * dt)
    lbar_im = mag * jnp.sin(lam_im * dt)
    den = lam_re * lam_re + lam_im * lam_im
    q_re = ((lbar_re - 1.0) * lam_re + lbar_im * lam_im) / den
    q_im = (lbar_im * lam_re - (lbar_re - 1.0) * lam_im) / den
    bbar_re = q_re[..., None] * b_re - q_im[..., None] * b_im
    bbar_im = q_re[..., None] * b_im + q_im[..., None] * b_re

    n, w = SSM_STATE, SSM_GROUP
    lam = jnp.stack([lbar_re, lbar_im]).reshape(2, 2, HALF_GROUPS, n)
    lam = jnp.broadcast_to(lam.transpose(1, 0, 2, 3).reshape(1, 4 * HALF_ST),
                           (SUBLANES, 4 * HALF_ST))
    tb = jnp.stack([bbar_re, bbar_im], axis=1).reshape(2, HALF_GROUPS, 2, n, w)
    tb = tb.transpose(0, 1, 4, 2, 3).reshape(2, HALF_IN, 2 * n)
    tc = jnp.stack([c_re, -c_im]).reshape(2, 2, HALF_GROUPS, w, n)
    tc = tc.transpose(1, 0, 2, 4, 3).reshape(2, 2 * HALF_ST, w)

    col = np.arange(2 * HALF_ST)
    ch = np.arange(HALF_IN)
    k = np.arange(2 * n)
    spread_b = (k[:, None] // n == col[None, :] // HALF_ST) & (k[:, None] % n == col[None, :] % n)
    own_b = ch[:, None] // w == (col[None, :] % HALF_ST) // n
    spread_c = np.arange(w)[:, None] == ch[None, :] % w
    bd = jnp.einsum("hrk,kc->hrc", tb, jnp.asarray(spread_b, F32)) * jnp.asarray(own_b, F32)
    cd = jnp.einsum("hrk,kc->hrc", tc, jnp.asarray(spread_c, F32)) * jnp.asarray(own_b.T, F32)
    return lam, bd.astype(BF16), cd.astype(BF16)


def _row_permutations():
    dst = np.arange(PERM_ROWS)
    src = (dst % BATCH) * PERM_TL + dst // BATCH
    perm = (src[:, None] == np.arange(PERM_ROWS)[None, :]).astype(np.float32)
    return (jnp.asarray(perm, dtype=BF16),
            jnp.asarray(np.concatenate([perm.T, perm.T], axis=1), dtype=BF16))


def kernel(x, p, ffn1_w_in, ffn1_w_out, ln1_g, ln1_b, mix_w_in, conv_w, conv_b, conv_w_out, ssm_lam_re, ssm_lam_im, ssm_log_step, ssm_b_re, ssm_b_im, ssm_c_re, ssm_c_im, ssm_d, ssm_w_glu, mix_w_out, ln2_g, ln2_b, ffn2_w_in, ffn2_w_out, ln3_g, ln3_b, ple_w_in, ple_w_gate, ln4_g, ln4_b):
    depth = p.shape[0]
    alpha = (2.0 * depth) ** 0.25
    assert x.shape == (BATCH, SEQ, D_MODEL) and BATCH == SUBLANES

    def row(v, i):
        return v[i:i + 1].reshape(1, -1)

    perm, unperm = _row_permutations()
    h = x.reshape(ROWS, D_MODEL)
    for i in range(depth):
        h, mw_in, cw_out, w_glu, mw_out = _ffn_ln(
            h, ffn1_w_in[i], ffn1_w_out[i], row(ln1_g, i),
            row(ln1_b, i), (mix_w_in[i], conv_w_out[i], ssm_w_glu[i], mix_w_out[i]), alpha)
        lam, bd, cd = _ssm_matrices(ssm_lam_re[i], ssm_lam_im[i], ssm_log_step[i],
                                    ssm_b_re[i], ssm_b_im[i], ssm_c_re[i], ssm_c_im[i])
        h, f2_in, f2_out, pw_in, pw_gate = _mixer(
            h.reshape(BATCH, SEQ, D_MODEL), mw_in, conv_w[i], row(conv_b, i), cw_out, lam,
            bd, cd, row(ssm_d, i), perm, unperm, w_glu, mw_out, row(ln2_g, i), row(ln2_b, i),
            (ffn2_w_in[i], ffn2_w_out[i], ple_w_in[i], ple_w_gate[i]), alpha)
        h = _ffn_ple(h.reshape(ROWS, D_MODEL), p[i].reshape(ROWS, PLE_DIM), f2_in, f2_out,
                     row(ln3_g, i), row(ln3_b, i), pw_in, pw_gate,
                     row(ln4_g, i), row(ln4_b, i), alpha)
    return h.reshape(BATCH, SEQ, D_MODEL)
```

```python
import functools

import jax
import jax.numpy as jnp
import numpy as np
from jax import lax
from jax.experimental import pallas as pl
from jax.experimental.pallas import tpu as pltpu

D_MODEL = 1024
BATCH = 8
SEQ = 2048
PLE_DIM = 256
D_FF = 2816
CONV_CH = 512
CONV_K = 3
SSM_WIDTH = 512
SSM_GROUP = 16
SSM_GROUPS = SSM_WIDTH // SSM_GROUP
SSM_STATE = 64
LN_EPS = 1e-5

SUBLANES = 8
LANES = 128
BF16_ROWS = 16
VMEM_LIMIT_BYTES = 56 * 1024 * 1024

ROWS = BATCH * SEQ
TM_FFN = 512
TL_MIX = 64
TM_MIX = BATCH * TL_MIX
FF_CHUNKS = ((0, 768), (768, 768), (1536, 768), (2304, 512))
FF_ANCHORS = 2
W_CAST_STEPS = 8
HALF_GROUPS = SSM_GROUPS // 2
HALF_IN = HALF_GROUPS * SSM_GROUP
HALF_ST = HALF_GROUPS * SSM_STATE
CONV_PAD = SUBLANES
MXU_DIM = 256
PERM_ROWS = MXU_DIM
PERM_TL = PERM_ROWS // BATCH

BF16 = jnp.bfloat16
F32 = jnp.float32


def _dot(a, b):
    return jnp.dot(a, b, preferred_element_type=F32)


def _layer_norm(y, g, b):
    mu = jnp.mean(y, axis=-1, keepdims=True)
    yc = y - mu
    var = jnp.mean(yc * yc, axis=-1, keepdims=True)
    return yc * lax.rsqrt(var + LN_EPS) * g + b


def _after(a, done):
    assert a.shape[0] == done.shape[0]
    dep = done[:, 0:LANES]
    for c in range(LANES, done.shape[1], LANES):
        dep = jnp.maximum(dep, done[:, c:c + LANES])
    dep = jnp.concatenate([dep] * (a.shape[1] // LANES), axis=1)
    return jnp.where(pl.program_id(0) >= 0, a, dep)


def _swiglu(xb, win_ref, wo_ref, finish):
    rows = xb.shape[0] // FF_ANCHORS
    acc = None
    pending = None
    for k, (c0, cw) in enumerate(FF_CHUNKS):
        gate = _dot(xb, win_ref[:, c0:c0 + cw])
        up = _dot(xb, win_ref[:, D_FF + c0:D_FF + c0 + cw])
        if k == 0:
            done = finish()
        elif k <= FF_ANCHORS:
            up = jnp.concatenate(
                [_after(up[r0:r0 + rows], done[r0:r0 + rows]) if r0 == (k - 1) * rows
                 else up[r0:r0 + rows] for r0 in range(0, xb.shape[0], rows)], axis=0)
        if pending is not None:
            h, rows_out = pending
            part = _dot(h, wo_ref[rows_out, :])
            acc = part if acc is None else acc + part
        pending = ((gate * jax.nn.sigmoid(gate) * up).astype(BF16), slice(c0, c0 + cw))
    h, rows_out = pending
    return acc + _dot(h, wo_ref[rows_out, :])


def _time_blocks(a, tl):
    return [jnp.concatenate([a[b * tl + t0:b * tl + t0 + PERM_TL] for b in range(BATCH)], axis=0)
            for t0 in range(0, tl, PERM_TL)]


def _cast_blocks(src_refs, dst_refs):
    for src, dst in zip(src_refs, dst_refs, strict=True):
        dst[...] = src[...].astype(BF16)


def _two_stage(n_tiles, pre_ref, matmul_stage, finish_stage, first_step=0):
    step = pl.program_id(0) - first_step

    @pl.when(step == 0)
    def _():
        pre_ref[...] = jnp.zeros_like(pre_ref)

    @pl.when((step >= 0) & (step < n_tiles))
    def _():
        pre_ref[...] = matmul_stage(finish_stage)

    @pl.when(step == n_tiles)
    def _():
        finish_stage()


def _ffn_ln_kernel(alpha, n_cast, n_tiles, x_ref, win_f32_ref, wo_f32_ref, g_ref, b_ref, *rest):
    cast_src, (o_ref, *cast_dst) = rest[:n_cast], rest[n_cast:-3]
    win_ref, wo_ref, pre_ref = rest[-3:]

    step = pl.program_id(0)

    @pl.when(step < W_CAST_STEPS)
    def _():
        for src, dst in ((win_f32_ref, win_ref), (wo_f32_ref, wo_ref)):
            rows = src.shape[0]
            dst[pl.ds(pl.multiple_of(step * rows, rows), rows), :] = src[...].astype(BF16)

    def matmul_stage(finish):
        _cast_blocks(cast_src, cast_dst)
        f = _swiglu(x_ref[...].astype(BF16), win_ref, wo_ref, finish)
        return alpha * x_ref[...] + 0.5 * f

    def finish_stage():
        out = _layer_norm(pre_ref[...], g_ref[...], b_ref[...])
        o_ref[...] = out
        return out

    _two_stage(n_tiles, pre_ref, matmul_stage, finish_stage, first_step=W_CAST_STEPS)


def _ffn_ple_kernel(alpha, n_tiles, x_ref, p_ref, win_ref, wo_ref, g3_ref, b3_ref,
                    wpi_ref, wpg_ref, g4_ref, b4_ref, o_ref, pre_ref):
    def matmul_stage(finish):
        f = _swiglu(x_ref[...].astype(BF16), win_ref, wo_ref, finish)
        return alpha * x_ref[...] + 0.5 * f

    def finish_stage():
        x3 = _layer_norm(pre_ref[...], g3_ref[...], b3_ref[...])
        e = _dot(p_ref[...].astype(BF16), wpi_ref[...]) * jax.nn.sigmoid(
            _dot(x3.astype(BF16), wpg_ref[...]))
        out = _layer_norm(alpha * x3 + e, g4_ref[...], b4_ref[...])
        o_ref[...] = out
        return out

    _two_stage(n_tiles, pre_ref, matmul_stage, finish_stage)


def _mixer_kernel(alpha, n_cast, n_tiles, x_ref, win_ref, cw_ref, cb_ref, cwo_ref, lam_ref,
                  bd_ref, cd_ref, dsk_ref, perm_ref, unperm_ref, wglu_ref, wout_ref, g_ref,
                  b_ref, *rest):
    cast_src, o_ref = rest[:n_cast], rest[n_cast]
    cast_dst = rest[n_cast + 1:2 * n_cast + 1]
    s_ref, sb_ref, st_ref, v_ref, pre_ref = rest[2 * n_cast + 1:]
    tl = x_ref.shape[1]
    tm = BATCH * tl
    step = pl.program_id(0)

    @pl.when(step == 0)
    def _():
        st_ref[...] = jnp.zeros_like(st_ref)
        v_ref[:, 0:CONV_PAD, :] = jnp.zeros((BATCH, CONV_PAD, CONV_CH), F32)

    def finish_stage():
        out = _layer_norm(pre_ref[...], g_ref[...], b_ref[...])
        o_ref[...] = out.reshape(BATCH, tl, D_MODEL)
        return out

    def matmul_stage(finish):
        _cast_blocks(cast_src, cast_dst)
        return _mixer_matmul_stage(
            alpha, tl, x_ref, win_ref, cw_ref, cb_ref, cwo_ref, lam_ref, bd_ref, cd_ref,
            dsk_ref, perm_ref, unperm_ref, wglu_ref, wout_ref, s_ref, sb_ref, st_ref, v_ref,
            finish)

    _two_stage(n_tiles, pre_ref, matmul_stage, finish_stage)


def _mixer_matmul_stage(alpha, tl, x_ref, win_ref, cw_ref, cb_ref, cwo_ref, lam_ref, bd_ref,
                        cd_ref, dsk_ref, perm_ref, unperm_ref, wglu_ref, wout_ref, s_ref,
                        sb_ref, st_ref, v_ref, finish):
    tm = BATCH * tl
    xb = x_ref[...].reshape(tm, D_MODEL).astype(BF16)

    def proj(c0, cw):
        return _dot(xb, win_ref[:, c0:c0 + cw])

    n_blocks = tl // PERM_TL
    cols = [(slice(2 * h * HALF_ST, (2 * h + 1) * HALF_ST),
             slice((2 * h + 1) * HALF_ST, (2 * h + 2) * HALF_ST)) for h in range(2)]
    block_rows = [slice(j * PERM_ROWS, (j + 1) * PERM_ROWS) for j in range(n_blocks)]

    half = D_MODEL // 2
    u_blocks = _time_blocks(proj(3 * CONV_CH, SSM_WIDTH), tl)
    u_blocks = [_after(u, d) for u, d in zip(u_blocks, _time_blocks(finish(), tl), strict=True)]
    gate_conv = [proj(4 * CONV_CH, half)]
    ub_blocks = [_dot(perm_ref[...], u.astype(BF16)).astype(BF16) for u in u_blocks]

    def b_proj(j, h):
        s_ref[block_rows[j], 2 * h * HALF_ST:2 * (h + 1) * HALF_ST] = _dot(
            ub_blocks[j][:, h * HALF_IN:(h + 1) * HALF_IN], bd_ref[h])

    def scan_block(j, state):
        out = []
        for (re, im), (sr, si) in zip(cols, state, strict=True):
            lam_re, lam_im = lam_ref[:, re], lam_ref[:, im]
            for l in range(j * PERM_TL, (j + 1) * PERM_TL, BF16_ROWS // SUBLANES):
                pair_re, pair_im = [], []
                for t in range(l, l + BF16_ROWS // SUBLANES):
                    rows = slice(t * SUBLANES, (t + 1) * SUBLANES)
                    sr, si = (lam_re * sr - lam_im * si + s_ref[rows, re],
                              lam_re * si + lam_im * sr + s_ref[rows, im])
                    pair_re.append(sr)
                    pair_im.append(si)
                rows = slice(l * SUBLANES, l * SUBLANES + BF16_ROWS)
                sb_ref[rows, re] = jnp.concatenate(pair_re, axis=0).astype(BF16)
                sb_ref[rows, im] = jnp.concatenate(pair_im, axis=0).astype(BF16)
            out.append((sr, si))
        return out

    def c_proj(j):
        return jnp.concatenate(
            [_dot(sb_ref[block_rows[j], 2 * h * HALF_ST:2 * (h + 1) * HALF_ST], cd_ref[h])
             for h in range(2)], axis=-1)

    assert n_blocks == 2
    state = [(st_ref[:, re], st_ref[:, im]) for re, im in cols]
    b_proj(0, 0)
    gate_conv.append(proj(4 * CONV_CH + half, half))
    b_proj(0, 1)
    gate_ssm = [proj(4 * CONV_CH + D_MODEL, half)]
    state = scan_block(0, state)
    b_proj(1, 0)
    gate_ssm.append(proj(4 * CONV_CH + D_MODEL + half, half))
    b_proj(1, 1)

    v = proj(CONV_CH, CONV_CH) * proj(2 * CONV_CH, CONV_CH)
    v_ref[:, CONV_PAD:CONV_PAD + tl, :] = v.reshape(BATCH, tl, CONV_CH)
    z = cb_ref[...].reshape(1, 1, CONV_CH)
    for k in range(CONV_K):
        lag = CONV_K - 1 - k
        z = z + cw_ref[k:k + 1, :].reshape(1, 1, CONV_CH) * v_ref[
            :, CONV_PAD - lag:CONV_PAD - lag + tl, :]
    v_ref[:, 0:CONV_PAD, :] = v_ref[:, tl:tl + CONV_PAD, :]
    conv_gated = (proj(0, CONV_CH) * z.reshape(tm, CONV_CH)).astype(BF16)
    y_tb_blocks = [c_proj(0)]
    state = scan_block(1, state)
    y_conv = _dot(conv_gated, cwo_ref[...])
    y_tb_blocks.append(c_proj(1))
    for (re, im), (sr, si) in zip(cols, state, strict=True):
        st_ref[:, re] = sr
        st_ref[:, im] = si

    gate_blocks = _time_blocks(jax.nn.sigmoid(jnp.concatenate(gate_ssm, axis=1)), tl)
    merged_blocks = _time_blocks(jax.nn.sigmoid(jnp.concatenate(gate_conv, axis=1)) * y_conv, tl)
    y_blocks = []
    for y_tb, u, gate in zip(y_tb_blocks, u_blocks, gate_blocks, strict=True):
        y_tb = _after(y_tb, gate)
        hi = y_tb.astype(BF16)
        lo = (y_tb - hi.astype(F32)).astype(BF16)
        y_blocks.append(
            _dot(unperm_ref[...], jnp.concatenate([hi, lo], axis=0)) + dsk_ref[...] * u)
    sg_blocks = [jax.nn.gelu(_after(y, mc)).astype(BF16)
                 for y, mc in zip(y_blocks, merged_blocks, strict=True)]
    glu_a = [_dot(sg, wglu_ref[:, 0:D_MODEL]) for sg in sg_blocks]
    glu_b = [_dot(sg, wglu_ref[:, D_MODEL:2 * D_MODEL]) for sg in sg_blocks]
    pre_blocks = []
    for j in range(n_blocks):
        m = merged_blocks[j] + gate_blocks[j] * (glu_a[j] * jax.nn.sigmoid(glu_b[j]))
        x = x_ref[:, j * PERM_TL:(j + 1) * PERM_TL, :].reshape(PERM_ROWS, D_MODEL)
        pre_blocks.append(alpha * x + _dot(m.astype(BF16), wout_ref[...]))
    return jnp.concatenate([blk[b * PERM_TL:(b + 1) * PERM_TL]
                            for b in range(BATCH) for blk in pre_blocks], axis=0)


def _resident(shape):
    zeros = (0,) * len(shape)
    return pl.BlockSpec(shape, lambda i: zeros, pipeline_mode=pl.Buffered(1))


def _matmul_tile(n_tiles, first_step=0):
    return lambda i: jnp.clip(i - first_step, 0, n_tiles - 1)


def _finish_tile(first_step=0):
    return lambda i: jnp.maximum(i - first_step - 1, 0)


def _rows(tm, width, tile_of_step):
    return pl.BlockSpec((tm, width), lambda i: (tile_of_step(i), 0))


def _cast_plan(weights, n_tiles, first_step=0):
    specs, shapes = [], []
    tile = _matmul_tile(n_tiles, first_step)
    for w in weights:
        rows, cols = w.shape
        n_blocks = n_tiles
        while (rows // n_blocks) % BF16_ROWS or rows % n_blocks:
            n_blocks //= 2
        repeat = n_tiles // n_blocks
        specs.append(pl.BlockSpec((rows // n_blocks, cols),
                                  lambda i, repeat=repeat: (tile(i) // repeat, 0)))
        shapes.append(jax.ShapeDtypeStruct(w.shape, BF16))
    return specs, shapes


def _ffn_ln(x, win, wo, g, b, next_weights, alpha):
    n_tiles = ROWS // TM_FFN
    first = W_CAST_STEPS
    cast_specs, cast_shapes = _cast_plan(next_weights, n_tiles, first)

    def own_weight(w):
        return pl.BlockSpec((w.shape[0] // first, w.shape[1]),
                            lambda i: (jnp.minimum(i, first - 1), 0))

    return pl.pallas_call(
        functools.partial(_ffn_ln_kernel, alpha, len(next_weights), n_tiles),
        grid=(first + n_tiles + 1,),
        in_specs=[_rows(TM_FFN, D_MODEL, _matmul_tile(n_tiles, first)), own_weight(win),
                  own_weight(wo), _resident(g.shape), _resident(b.shape)] + cast_specs,
        out_specs=[_rows(TM_FFN, D_MODEL, _finish_tile(first))] + cast_specs,
        out_shape=[jax.ShapeDtypeStruct((ROWS, D_MODEL), F32)] + cast_shapes,
        scratch_shapes=[pltpu.VMEM(win.shape, BF16), pltpu.VMEM(wo.shape, BF16),
                        pltpu.VMEM((TM_FFN, D_MODEL), F32)],
        compiler_params=pltpu.CompilerParams(
            dimension_semantics=("arbitrary",), vmem_limit_bytes=VMEM_LIMIT_BYTES),
        name="ffn_ln",
    )(x, win, wo, g, b, *next_weights)


def _ffn_ple(x, p, win, wo, g3, b3, wpi, wpg, g4, b4, alpha):
    n_tiles = ROWS // TM_FFN
    return pl.pallas_call(
        functools.partial(_ffn_ple_kernel, alpha, n_tiles),
        grid=(n_tiles + 1,),
        in_specs=[_rows(TM_FFN, D_MODEL, _matmul_tile(n_tiles)),
                  _rows(TM_FFN, PLE_DIM, _finish_tile()), _resident(win.shape),
                  _resident(wo.shape), _resident(g3.shape),
                  _resident(b3.shape), _resident(wpi.shape), _resident(wpg.shape),
                  _resident(g4.shape), _resident(b4.shape)],
        out_specs=_rows(TM_FFN, D_MODEL, _finish_tile()),
        out_shape=jax.ShapeDtypeStruct((ROWS, D_MODEL), F32),
        scratch_shapes=[pltpu.VMEM((TM_FFN, D_MODEL), F32)],
        compiler_params=pltpu.CompilerParams(
            dimension_semantics=("arbitrary",), vmem_limit_bytes=VMEM_LIMIT_BYTES),
        name="ffn_ple",
    )(x, p, win, wo, g3, b3, wpi, wpg, g4, b4)


def _mixer(x, win, cw, cb, cwo, lam, bd, cd, dsk, perm, unperm, wglu, wout, g, b,
           next_weights, alpha):
    consts = (win, cw, cb, cwo, lam, bd, cd, dsk, perm, unperm, wglu, wout, g, b)
    n_tiles = SEQ // TL_MIX
    cast_specs, cast_shapes = _cast_plan(next_weights, n_tiles)
    matmul_tile, finish_tile = _matmul_tile(n_tiles), _finish_tile()
    block = (BATCH, TL_MIX, D_MODEL)
    return pl.pallas_call(
        functools.partial(_mixer_kernel, alpha, len(next_weights), n_tiles),
        grid=(n_tiles + 1,),
        in_specs=[pl.BlockSpec(block, lambda i: (0, matmul_tile(i), 0))]
        + [_resident(c.shape) for c in consts] + cast_specs,
        out_specs=[pl.BlockSpec(block, lambda i: (0, finish_tile(i), 0))] + cast_specs,
        out_shape=[jax.ShapeDtypeStruct((BATCH, SEQ, D_MODEL), F32)] + cast_shapes,
        scratch_shapes=[
            pltpu.VMEM((TM_MIX, 4 * HALF_ST), F32),
            pltpu.VMEM((TM_MIX, 4 * HALF_ST), BF16),
            pltpu.VMEM((SUBLANES, 4 * HALF_ST), F32),
            pltpu.VMEM((BATCH, CONV_PAD + TL_MIX, CONV_CH), F32),
            pltpu.VMEM((TM_MIX, D_MODEL), F32),
        ],
        compiler_params=pltpu.CompilerParams(
            dimension_semantics=("arbitrary",), vmem_limit_bytes=VMEM_LIMIT_BYTES),
        name="mixer",
    )(x, *consts, *next_weights)


def _ssm_matrices(lam_re, lam_im, log_step, b_re, b_im, c_re, c_im):
    dt = jnp.exp(log_step)[:, None]
    mag = jnp.exp(lam_re * dt)
    lbar_re = mag * jnp.cos(lam_im * dt)
    lbar_im = mag * jnp.sin(lam_im * dt)
    den = lam_re * lam_re + lam_im * lam_im
    q_re = ((lbar_re - 1.0) * lam_re + lbar_im * lam_im) / den
    q_im = (lbar_im * lam_re - (lbar_re - 1.0) * lam_im) / den
    bbar_re = q_re[..., None] * b_re - q_im[..., None] * b_im
    bbar_im = q_re[..., None] * b_im + q_im[..., None] * b_re

    n, w = SSM_STATE, SSM_GROUP
    lam = jnp.stack([lbar_re, lbar_im]).reshape(2, 2, HALF_GROUPS, n)
    lam = jnp.broadcast_to(lam.transpose(1, 0, 2, 3).reshape(1, 4 * HALF_ST),
                           (SUBLANES, 4 * HALF_ST))
    tb = jnp.stack([bbar_re, bbar_im], axis=1).reshape(2, HALF_GROUPS, 2, n, w)
    tb = tb.transpose(0, 1, 4, 2, 3).reshape(2, HALF_IN, 2 * n)
    tc = jnp.stack([c_re, -c_im]).reshape(2, 2, HALF_GROUPS, w, n)
    tc = tc.transpose(1, 0, 2, 4, 3).reshape(2, 2 * HALF_ST, w)

    col = np.arange(2 * HALF_ST)
    ch = np.arange(HALF_IN)
    k = np.arange(2 * n)
    spread_b = (k[:, None] // n == col[None, :] // HALF_ST) & (k[:, None] % n == col[None, :] % n)
    own_b = ch[:, None] // w == (col[None, :] % HALF_ST) // n
    spread_c = np.arange(w)[:, None] == ch[None, :] % w
    bd = jnp.einsum("hrk,kc->hrc", tb, jnp.asarray(spread_b, F32)) * jnp.asarray(own_b, F32)
    cd = jnp.einsum("hrk,kc->hrc", tc, jnp.asarray(spread_c, F32)) * jnp.asarray(own_b.T, F32)
    return lam, bd.astype(BF16), cd.astype(BF16)


def _row_permutations():
    dst = np.arange(PERM_ROWS)
    src = (dst % BATCH) * PERM_TL + dst // BATCH
    perm = (src[:, None] == np.arange(PERM_ROWS)[None, :]).astype(np.float32)
    return (jnp.asarray(perm, dtype=BF16),
            jnp.asarray(np.concatenate([perm.T, perm.T], axis=1), dtype=BF16))


def kernel(x, p, ffn1_w_in, ffn1_w_out, ln1_g, ln1_b, mix_w_in, conv_w, conv_b, conv_w_out, ssm_lam_re, ssm_lam_im, ssm_log_step, ssm_b_re, ssm_b_im, ssm_c_re, ssm_c_im, ssm_d, ssm_w_glu, mix_w_out, ln2_g, ln2_b, ffn2_w_in, ffn2_w_out, ln3_g, ln3_b, ple_w_in, ple_w_gate, ln4_g, ln4_b):
    depth = p.shape[0]
    alpha = (2.0 * depth) ** 0.25
    assert x.shape == (BATCH, SEQ, D_MODEL) and BATCH == SUBLANES

    def row(v, i):
        return v[i:i + 1].reshape(1, -1)

    perm, unperm = _row_permutations()
    h = x.reshape(ROWS, D_MODEL)
    for i in range(depth):
        h, mw_in, cw_out, w_glu, mw_out = _ffn_ln(
            h, ffn1_w_in[i], ffn1_w_out[i], row(ln1_g, i),
            row(ln1_b, i), (mix_w_in[i], conv_w_out[i], ssm_w_glu[i], mix_w_out[i]), alpha)
        lam, bd, cd = _ssm_matrices(ssm_lam_re[i], ssm_lam_im[i], ssm_log_step[i],
                                    ssm_b_re[i], ssm_b_im[i], ssm_c_re[i], ssm_c_im[i])
        h, f2_in, f2_out, pw_in, pw_gate = _mixer(
            h.reshape(BATCH, SEQ, D_MODEL), mw_in, conv_w[i], row(conv_b, i), cw_out, lam,
            bd, cd, row(ssm_d, i), perm, unperm, w_glu, mw_out, row(ln2_g, i), row(ln2_b, i),
            (ffn2_w_in[i], ffn2_w_out[i], ple_w_in[i], ple_w_gate[i]), alpha)
        h = _ffn_ple(h.reshape(ROWS, D_MODEL), p[i].reshape(ROWS, PLE_DIM), f2_in, f2_out,
                     row(ln3_g, i), row(ln3_b, i), pw_in, pw_gate,
                     row(ln4_g, i), row(ln4_b, i), alpha)
    return h.reshape(BATCH, SEQ, D_MODEL)
```

```python
import functools

import jax
import jax.numpy as jnp
import numpy as np
from jax import lax
from jax.experimental import pallas as pl
from jax.experimental.pallas import tpu as pltpu

D_MODEL = 1024
BATCH = 8
SEQ = 2048
PLE_DIM = 256
D_FF = 2816
CONV_CH = 512
CONV_K = 3
SSM_WIDTH = 512
SSM_GROUP = 16
SSM_GROUPS = SSM_WIDTH // SSM_GROUP
SSM_STATE = 64
LN_EPS = 1e-5

SUBLANES = 8
LANES = 128
BF16_ROWS = 16
VMEM_LIMIT_BYTES = 56 * 1024 * 1024

ROWS = BATCH * SEQ
TM_FFN = 512
TL_MIX = 64
TM_MIX = BATCH * TL_MIX
FF_CHUNKS = ((0, 768), (768, 768), (1536, 768), (2304, 512))
FF_ANCHORS = 2
W_CAST_STEPS = 8
HALF_GROUPS = SSM_GROUPS // 2
HALF_IN = HALF_GROUPS * SSM_GROUP
HALF_ST = HALF_GROUPS * SSM_STATE
CONV_PAD = SUBLANES
MXU_DIM = 256
PERM_ROWS = MXU_DIM
PERM_TL = PERM_ROWS // BATCH

BF16 = jnp.bfloat16
F32 = jnp.float32


def _dot(a, b):
    return jnp.dot(a, b, preferred_element_type=F32)


def _layer_norm(y, g, b):
    mu = jnp.mean(y, axis=-1, keepdims=True)
    yc = y - mu
    var = jnp.mean(yc * yc, axis=-1, keepdims=True)
    return yc * lax.rsqrt(var + LN_EPS) * g + b


def _after(a, done):
    assert a.shape[0] == done.shape[0]
    dep = done[:, 0:LANES]
    for c in range(LANES, done.shape[1], LANES):
        dep = jnp.maximum(dep, done[:, c:c + LANES])
    head = jnp.where(pl.program_id(0) >= 0, a[:, 0:LANES], dep)
    return jnp.concatenate([head, a[:, LANES:]], axis=1)


def _swiglu(xb, win_ref, wo_ref, finish):
    rows = xb.shape[0] // FF_ANCHORS
    acc = None
    pending = None
    for k, (c0, cw) in enumerate(FF_CHUNKS):
        gate = _dot(xb, win_ref[:, c0:c0 + cw])
        up = _dot(xb, win_ref[:, D_FF + c0:D_FF + c0 + cw])
        if k == 0:
            done = finish()
        elif k <= FF_ANCHORS:
            up = jnp.concatenate(
                [_after(up[r0:r0 + rows], done[r0:r0 + rows]) if r0 == (k - 1) * rows
                 else up[r0:r0 + rows] for r0 in range(0, xb.shape[0], rows)], axis=0)
        if pending is not None:
            h, rows_out = pending
            part = _dot(h, wo_ref[rows_out, :])
            acc = part if acc is None else acc + part
        pending = ((gate * jax.nn.sigmoid(gate) * up).astype(BF16), slice(c0, c0 + cw))
    h, rows_out = pending
    return acc + _dot(h, wo_ref[rows_out, :])


def _time_blocks(a, tl):
    return [jnp.concatenate([a[b * tl + t0:b * tl + t0 + PERM_TL] for b in range(BATCH)], axis=0)
            for t0 in range(0, tl, PERM_TL)]


def _cast_blocks(src_refs, dst_refs):
    for src, dst in zip(src_refs, dst_refs, strict=True):
        dst[...] = src[...].astype(BF16)


def _two_stage(n_tiles, pre_ref, matmul_stage, finish_stage, first_step=0):
    step = pl.program_id(0) - first_step

    @pl.when(step == 0)
    def _():
        pre_ref[...] = jnp.zeros_like(pre_ref)

    @pl.when((step >= 0) & (step < n_tiles))
    def _():
        pre_ref[...] = matmul_stage(finish_stage)

    @pl.when(step == n_tiles)
    def _():
        finish_stage()


def _ffn_ln_kernel(alpha, n_cast, n_tiles, x_ref, win_f32_ref, wo_f32_ref, g_ref, b_ref, *rest):
    cast_src, (o_ref, *cast_dst) = rest[:n_cast], rest[n_cast:-3]
    win_ref, wo_ref, pre_ref = rest[-3:]

    step = pl.program_id(0)

    @pl.when(step < W_CAST_STEPS)
    def _():
        for src, dst in ((win_f32_ref, win_ref), (wo_f32_ref, wo_ref)):
            rows = src.shape[0]
            dst[pl.ds(pl.multiple_of(step * rows, rows), rows), :] = src[...].astype(BF16)

    def matmul_stage(finish):
        _cast_blocks(cast_src, cast_dst)
        f = _swiglu(x_ref[...].astype(BF16), win_ref, wo_ref, finish)
        return alpha * x_ref[...] + 0.5 * f

    def finish_stage():
        out = _layer_norm(pre_ref[...], g_ref[...], b_ref[...])
        o_ref[...] = out
        return out

    _two_stage(n_tiles, pre_ref, matmul_stage, finish_stage, first_step=W_CAST_STEPS)


def _ffn_ple_kernel(alpha, n_tiles, x_ref, p_ref, win_ref, wo_ref, g3_ref, b3_ref,
                    wpi_ref, wpg_ref, g4_ref, b4_ref, o_ref, pre_ref):
    def matmul_stage(finish):
        f = _swiglu(x_ref[...].astype(BF16), win_ref, wo_ref, finish)
        return alpha * x_ref[...] + 0.5 * f

    def finish_stage():
        x3 = _layer_norm(pre_ref[...], g3_ref[...], b3_ref[...])
        e = _dot(p_ref[...].astype(BF16), wpi_ref[...]) * jax.nn.sigmoid(
            _dot(x3.astype(BF16), wpg_ref[...]))
        out = _layer_norm(alpha * x3 + e, g4_ref[...], b4_ref[...])
        o_ref[...] = out
        return out

    _two_stage(n_tiles, pre_ref, matmul_stage, finish_stage)


def _mixer_kernel(alpha, n_cast, n_tiles, x_ref, win_ref, cw_ref, cb_ref, cwo_ref, lam_ref,
                  bd_ref, cd_ref, dsk_ref, perm_ref, unperm_ref, wglu_ref, wout_ref, g_ref,
                  b_ref, *rest):
    cast_src, o_ref = rest[:n_cast], rest[n_cast]
    cast_dst = rest[n_cast + 1:2 * n_cast + 1]
    s_ref, sb_ref, st_ref, v_ref, pre_ref = rest[2 * n_cast + 1:]
    tl = x_ref.shape[1]
    tm = BATCH * tl
    step = pl.program_id(0)

    @pl.when(step == 0)
    def _():
        st_ref[...] = jnp.zeros_like(st_ref)
        v_ref[:, 0:CONV_PAD, :] = jnp.zeros((BATCH, CONV_PAD, CONV_CH), F32)

    def finish_stage():
        out = _layer_norm(pre_ref[...], g_ref[...], b_ref[...])
        o_ref[...] = out.reshape(BATCH, tl, D_MODEL)
        return out

    def matmul_stage(finish):
        _cast_blocks(cast_src, cast_dst)
        return _mixer_matmul_stage(
            alpha, tl, x_ref, win_ref, cw_ref, cb_ref, cwo_ref, lam_ref, bd_ref, cd_ref,
            dsk_ref, perm_ref, unperm_ref, wglu_ref, wout_ref, s_ref, sb_ref, st_ref, v_ref,
            finish)

    _two_stage(n_tiles, pre_ref, matmul_stage, finish_stage)


def _mixer_matmul_stage(alpha, tl, x_ref, win_ref, cw_ref, cb_ref, cwo_ref, lam_ref, bd_ref,
                        cd_ref, dsk_ref, perm_ref, unperm_ref, wglu_ref, wout_ref, s_ref,
                        sb_ref, st_ref, v_ref, finish):
    tm = BATCH * tl
    xb = x_ref[...].reshape(tm, D_MODEL).astype(BF16)

    def proj(c0, cw):
        return _dot(xb, win_ref[:, c0:c0 + cw])

    n_blocks = tl // PERM_TL
    cols = [(slice(2 * h * HALF_ST, (2 * h + 1) * HALF_ST),
             slice((2 * h + 1) * HALF_ST, (2 * h + 2) * HALF_ST)) for h in range(2)]
    block_rows = [slice(j * PERM_ROWS, (j + 1) * PERM_ROWS) for j in range(n_blocks)]

    half = D_MODEL // 2
    u_blocks = _time_blocks(proj(3 * CONV_CH, SSM_WIDTH), tl)
    u_blocks = [_after(u, d) for u, d in zip(u_blocks, _time_blocks(finish(), tl), strict=True)]
    gate_conv = [proj(4 * CONV_CH, half)]
    ub_blocks = [_dot(perm_ref[...], u.astype(BF16)).astype(BF16) for u in u_blocks]

    def b_proj(j, h):
        s_ref[block_rows[j], 2 * h * HALF_ST:2 * (h + 1) * HALF_ST] = _dot(
            ub_blocks[j][:, h * HALF_IN:(h + 1) * HALF_IN], bd_ref[h])

    def scan_block(j, state):
        out = []
        for (re, im), (sr, si) in zip(cols, state, strict=True):
            lam_re, lam_im = lam_ref[:, re], lam_ref[:, im]
            for l in range(j * PERM_TL, (j + 1) * PERM_TL, BF16_ROWS // SUBLANES):
                pair_re, pair_im = [], []
                for t in range(l, l + BF16_ROWS // SUBLANES):
                    rows = slice(t * SUBLANES, (t + 1) * SUBLANES)
                    sr, si = (lam_re * sr - lam_im * si + s_ref[rows, re],
                              lam_re * si + lam_im * sr + s_ref[rows, im])
                    pair_re.append(sr)
                    pair_im.append(si)
                rows = slice(l * SUBLANES, l * SUBLANES + BF16_ROWS)
                sb_ref[rows, re] = jnp.concatenate(pair_re, axis=0).astype(BF16)
                sb_ref[rows, im] = jnp.concatenate(pair_im, axis=0).astype(BF16)
            out.append((sr, si))
        return out

    def c_proj(j):
        return jnp.concatenate(
            [_dot(sb_ref[block_rows[j], 2 * h * HALF_ST:2 * (h + 1) * HALF_ST], cd_ref[h])
             for h in range(2)], axis=-1)

    assert n_blocks == 2
    state = [(st_ref[:, re], st_ref[:, im]) for re, im in cols]
    b_proj(0, 0)
    gate_conv.append(proj(4 * CONV_CH + half, half))
    b_proj(0, 1)
    gate_ssm = [proj(4 * CONV_CH + D_MODEL, half)]
    state = scan_block(0, state)
    b_proj(1, 0)
    gate_ssm.append(proj(4 * CONV_CH + D_MODEL + half, half))
    b_proj(1, 1)

    v = proj(CONV_CH, CONV_CH) * proj(2 * CONV_CH, CONV_CH)
    v_ref[:, CONV_PAD:CONV_PAD + tl, :] = v.reshape(BATCH, tl, CONV_CH)
    z = cb_ref[...].reshape(1, 1, CONV_CH)
    for k in range(CONV_K):
        lag = CONV_K - 1 - k
        z = z + cw_ref[k:k + 1, :].reshape(1, 1, CONV_CH) * v_ref[
            :, CONV_PAD - lag:CONV_PAD - lag + tl, :]
    v_ref[:, 0:CONV_PAD, :] = v_ref[:, tl:tl + CONV_PAD, :]
    conv_gated = (proj(0, CONV_CH) * z.reshape(tm, CONV_CH)).astype(BF16)
    y_tb_blocks = [c_proj(0)]
    state = scan_block(1, state)
    y_conv = _dot(conv_gated, cwo_ref[...])
    y_tb_blocks.append(c_proj(1))
    for (re, im), (sr, si) in zip(cols, state, strict=True):
        st_ref[:, re] = sr
        st_ref[:, im] = si

    gate_blocks = _time_blocks(jax.nn.sigmoid(jnp.concatenate(gate_ssm, axis=1)), tl)
    merged_blocks = _time_blocks(jax.nn.sigmoid(jnp.concatenate(gate_conv, axis=1)) * y_conv, tl)
    y_blocks = []
    for y_tb, u, gate in zip(y_tb_blocks, u_blocks, gate_blocks, strict=True):
        y_tb = _after(y_tb, gate)
        hi = y_tb.astype(BF16)
        lo = (y_tb - hi.astype(F32)).astype(BF16)
        y_blocks.append(
            _dot(unperm_ref[...], jnp.concatenate([hi, lo], axis=0)) + dsk_ref[...] * u)
    sg_blocks = [jax.nn.gelu(_after(y, mc)).astype(BF16)
                 for y, mc in zip(y_blocks, merged_blocks, strict=True)]
    glu_a = [_dot(sg, wglu_ref[:, 0:D_MODEL]) for sg in sg_blocks]
    glu_b = [_dot(sg, wglu_ref[:, D_MODEL:2 * D_MODEL]) for sg in sg_blocks]
    pre_blocks = []
    for j in range(n_blocks):
        m = merged_blocks[j] + gate_blocks[j] * (glu_a[j] * jax.nn.sigmoid(glu_b[j]))
        x = x_ref[:, j * PERM_TL:(j + 1) * PERM_TL, :].reshape(PERM_ROWS, D_MODEL)
        pre_blocks.append(alpha * x + _dot(m.astype(BF16), wout_ref[...]))
    return jnp.concatenate([blk[b * PERM_TL:(b + 1) * PERM_TL]
                            for b in range(BATCH) for blk in pre_blocks], axis=0)


def _resident(shape):
    zeros = (0,) * len(shape)
    return pl.BlockSpec(shape, lambda i: zeros, pipeline_mode=pl.Buffered(1))


def _matmul_tile(n_tiles, first_step=0):
    return lambda i: jnp.clip(i - first_step, 0, n_tiles - 1)


def _finish_tile(first_step=0):
    return lambda i: jnp.maximum(i - first_step - 1, 0)


def _rows(tm, width, tile_of_step):
    return pl.BlockSpec((tm, width), lambda i: (tile_of_step(i), 0))


def _cast_plan(weights, n_tiles, first_step=0):
    specs, shapes = [], []
    tile = _matmul_tile(n_tiles, first_step)
    for w in weights:
        rows, cols = w.shape
        n_blocks = n_tiles
        while (rows // n_blocks) % BF16_ROWS or rows % n_blocks:
            n_blocks //= 2
        repeat = n_tiles // n_blocks
        specs.append(pl.BlockSpec((rows // n_blocks, cols),
                                  lambda i, repeat=repeat: (tile(i) // repeat, 0)))
        shapes.append(jax.ShapeDtypeStruct(w.shape, BF16))
    return specs, shapes


def _ffn_ln(x, win, wo, g, b, next_weights, alpha):
    n_tiles = ROWS // TM_FFN
    first = W_CAST_STEPS
    cast_specs, cast_shapes = _cast_plan(next_weights, n_tiles, first)

    def own_weight(w):
        return pl.BlockSpec((w.shape[0] // first, w.shape[1]),
                            lambda i: (jnp.minimum(i, first - 1), 0))

    return pl.pallas_call(
        functools.partial(_ffn_ln_kernel, alpha, len(next_weights), n_tiles),
        grid=(first + n_tiles + 1,),
        in_specs=[_rows(TM_FFN, D_MODEL, _matmul_tile(n_tiles, first)), own_weight(win),
                  own_weight(wo), _resident(g.shape), _resident(b.shape)] + cast_specs,
        out_specs=[_rows(TM_FFN, D_MODEL, _finish_tile(first))] + cast_specs,
        out_shape=[jax.ShapeDtypeStruct((ROWS, D_MODEL), F32)] + cast_shapes,
        scratch_shapes=[pltpu.VMEM(win.shape, BF16), pltpu.VMEM(wo.shape, BF16),
                        pltpu.VMEM((TM_FFN, D_MODEL), F32)],
        compiler_params=pltpu.CompilerParams(
            dimension_semantics=("arbitrary",), vmem_limit_bytes=VMEM_LIMIT_BYTES),
        name="ffn_ln",
    )(x, win, wo, g, b, *next_weights)


def _ffn_ple(x, p, win, wo, g3, b3, wpi, wpg, g4, b4, alpha):
    n_tiles = ROWS // TM_FFN
    return pl.pallas_call(
        functools.partial(_ffn_ple_kernel, alpha, n_tiles),
        grid=(n_tiles + 1,),
        in_specs=[_rows(TM_FFN, D_MODEL, _matmul_tile(n_tiles)),
                  _rows(TM_FFN, PLE_DIM, _finish_tile()), _resident(win.shape),
                  _resident(wo.shape), _resident(g3.shape),
                  _resident(b3.shape), _resident(wpi.shape), _resident(wpg.shape),
                  _resident(g4.shape), _resident(b4.shape)],
        out_specs=_rows(TM_FFN, D_MODEL, _finish_tile()),
        out_shape=jax.ShapeDtypeStruct((ROWS, D_MODEL), F32),
        scratch_shapes=[pltpu.VMEM((TM_FFN, D_MODEL), F32)],
        compiler_params=pltpu.CompilerParams(
            dimension_semantics=("arbitrary",), vmem_limit_bytes=VMEM_LIMIT_BYTES),
        name="ffn_ple",
    )(x, p, win, wo, g3, b3, wpi, wpg, g4, b4)


def _mixer(x, win, cw, cb, cwo, lam, bd, cd, dsk, perm, unperm, wglu, wout, g, b,
           next_weights, alpha):
    consts = (win, cw, cb, cwo, lam, bd, cd, dsk, perm, unperm, wglu, wout, g, b)
    n_tiles = SEQ // TL_MIX
    cast_specs, cast_shapes = _cast_plan(next_weights, n_tiles)
    matmul_tile, finish_tile = _matmul_tile(n_tiles), _finish_tile()
    block = (BATCH, TL_MIX, D_MODEL)
    return pl.pallas_call(
        functools.partial(_mixer_kernel, alpha, len(next_weights), n_tiles),
        grid=(n_tiles + 1,),
        in_specs=[pl.BlockSpec(block, lambda i: (0, matmul_tile(i), 0))]
        + [_resident(c.shape) for c in consts] + cast_specs,
        out_specs=[pl.BlockSpec(block, lambda i: (0, finish_tile(i), 0))] + cast_specs,
        out_shape=[jax.ShapeDtypeStruct((BATCH, SEQ, D_MODEL), F32)] + cast_shapes,
        scratch_shapes=[
            pltpu.VMEM((TM_MIX, 4 * HALF_ST), F32),
            pltpu.VMEM((TM_MIX, 4 * HALF_ST), BF16),
            pltpu.VMEM((SUBLANES, 4 * HALF_ST), F32),
            pltpu.VMEM((BATCH, CONV_PAD + TL_MIX, CONV_CH), F32),
            pltpu.VMEM((TM_MIX, D_MODEL), F32),
        ],
        compiler_params=pltpu.CompilerParams(
            dimension_semantics=("arbitrary",), vmem_limit_bytes=VMEM_LIMIT_BYTES),
        name="mixer",
    )(x, *consts, *next_weights)


def _ssm_matrices(lam_re, lam_im, log_step, b_re, b_im, c_re, c_im):
    dt = jnp.exp(log_step)[:, None]
    mag = jnp.exp(lam_re * dt)
    lbar_re = mag * jnp.cos(lam_im * dt)
    lbar_im = mag * jnp.sin(lam_im * dt)
    den = lam_re * lam_re + lam_im * lam_im
    q_re = ((lbar_re - 1.0) * lam_re + lbar_im * lam_im) / den
    q_im = (lbar_im * lam_re - (lbar_re - 1.0) * lam_im) / den
    bbar_re = q_re[..., None] * b_re - q_im[..., None] * b_im
    bbar_im = q_re[..., None] * b_im + q_im[..., None] * b_re

    n, w = SSM_STATE, SSM_GROUP
    lam = jnp.stack([lbar_re, lbar_im]).reshape(2, 2, HALF_GROUPS, n)
    lam = jnp.broadcast_to(lam.transpose(1, 0, 2, 3).reshape(1, 4 * HALF_ST),
                           (SUBLANES, 4 * HALF_ST))
    tb = jnp.stack([bbar_re, bbar_im], axis=1).reshape(2, HALF_GROUPS, 2, n, w)
    tb = tb.transpose(0, 1, 4, 2, 3).reshape(2, HALF_IN, 2 * n)
    tc = jnp.stack([c_re, -c_im]).reshape(2, 2, HALF_GROUPS, w, n)
    tc = tc.transpose(1, 0, 2, 4, 3).reshape(2, 2 * HALF_ST, w)

    col = np.arange(2 * HALF_ST)
    ch = np.arange(HALF_IN)
    k = np.arange(2 * n)
    spread_b = (k[:, None] // n == col[None, :] // HALF_ST) & (k[:, None] % n == col[None, :] % n)
    own_b = ch[:, None] // w == (col[None, :] % HALF_ST) // n
    spread_c = np.arange(w)[:, None] == ch[None, :] % w
    bd = jnp.einsum("hrk,kc->hrc", tb, jnp.asarray(spread_b, F32)) * jnp.asarray(own_b, F32)
    cd = jnp.einsum("hrk,kc->hrc", tc, jnp.asarray(spread_c, F32)) * jnp.asarray(own_b.T, F32)
    return lam, bd.astype(BF16), cd.astype(BF16)


def _row_permutations():
    dst = np.arange(PERM_ROWS)
    src = (dst % BATCH) * PERM_TL + dst // BATCH
    perm = (src[:, None] == np.arange(PERM_ROWS)[None, :]).astype(np.float32)
    return (jnp.asarray(perm, dtype=BF16),
            jnp.asarray(np.concatenate([perm.T, perm.T], axis=1), dtype=BF16))


def kernel(x, p, ffn1_w_in, ffn1_w_out, ln1_g, ln1_b, mix_w_in, conv_w, conv_b, conv_w_out, ssm_lam_re, ssm_lam_im, ssm_log_step, ssm_b_re, ssm_b_im, ssm_c_re, ssm_c_im, ssm_d, ssm_w_glu, mix_w_out, ln2_g, ln2_b, ffn2_w_in, ffn2_w_out, ln3_g, ln3_b, ple_w_in, ple_w_gate, ln4_g, ln4_b):
    depth = p.shape[0]
    alpha = (2.0 * depth) ** 0.25
    assert x.shape == (BATCH, SEQ, D_MODEL) and BATCH == SUBLANES

    def row(v, i):
        return v[i:i + 1].reshape(1, -1)

    perm, unperm = _row_permutations()
    h = x.reshape(ROWS, D_MODEL)
    for i in range(depth):
        h, mw_in, cw_out, w_glu, mw_out = _ffn_ln(
            h, ffn1_w_in[i], ffn1_w_out[i], row(ln1_g, i),
            row(ln1_b, i), (mix_w_in[i], conv_w_out[i], ssm_w_glu[i], mix_w_out[i]), alpha)
        lam, bd, cd = _ssm_matrices(ssm_lam_re[i], ssm_lam_im[i], ssm_log_step[i],
                                    ssm_b_re[i], ssm_b_im[i], ssm_c_re[i], ssm_c_im[i])
        h, f2_in, f2_out, pw_in, pw_gate = _mixer(
            h.reshape(BATCH, SEQ, D_MODEL), mw_in, conv_w[i], row(conv_b, i), cw_out, lam,
            bd, cd, row(ssm_d, i), perm, unperm, w_glu, mw_out, row(ln2_g, i), row(ln2_b, i),
            (ffn2_w_in[i], ffn2_w_out[i], ple_w_in[i], ple_w_gate[i]), alpha)
        h = _ffn_ple(h.reshape(ROWS, D_MODEL), p[i].reshape(ROWS, PLE_DIM), f2_in, f2_out,
                     row(ln3_g, i), row(ln3_b, i), pw_in, pw_gate,
                     row(ln4_g, i), row(ln4_b, i), alpha)
    return h.reshape(BATCH, SEQ, D_MODEL)
```

```python
import functools

import jax
import jax.numpy as jnp
import numpy as np
from jax import lax
from jax.experimental import pallas as pl
from jax.experimental.pallas import tpu as pltpu

D_MODEL = 1024
BATCH = 8
SEQ = 2048
PLE_DIM = 256
D_FF = 2816
CONV_CH = 512
CONV_K = 3
SSM_WIDTH = 512
SSM_GROUP = 16
SSM_GROUPS = SSM_WIDTH // SSM_GROUP
SSM_STATE = 64
LN_EPS = 1e-5

SUBLANES = 8
LANES = 128
BF16_ROWS = 16
VMEM_LIMIT_BYTES = 56 * 1024 * 1024

ROWS = BATCH * SEQ
TM_FFN = 1024
FFN_SUB_ROWS = tuple(slice(r, r + 512) for r in range(0, TM_FFN, 512))
TL_MIX = 64
TM_MIX = BATCH * TL_MIX
FF_CHUNKS = ((0, 768), (768, 768), (1536, 768), (2304, 512))
FF_ANCHORS = 2
W_CAST_STEPS = 8
HALF_GROUPS = SSM_GROUPS // 2
HALF_IN = HALF_GROUPS * SSM_GROUP
HALF_ST = HALF_GROUPS * SSM_STATE
CONV_PAD = SUBLANES
MXU_DIM = 256
PERM_ROWS = MXU_DIM
PERM_TL = PERM_ROWS // BATCH

BF16 = jnp.bfloat16
F32 = jnp.float32


def _dot(a, b):
    return jnp.dot(a, b, preferred_element_type=F32)


def _layer_norm(y, g, b):
    mu = jnp.mean(y, axis=-1, keepdims=True)
    yc = y - mu
    var = jnp.mean(yc * yc, axis=-1, keepdims=True)
    return yc * lax.rsqrt(var + LN_EPS) * g + b


def _after(a, done):
    assert a.shape[0] == done.shape[0]
    dep = done[:, 0:LANES]
    for c in range(LANES, done.shape[1], LANES):
        dep = jnp.maximum(dep, done[:, c:c + LANES])
    head = jnp.where(pl.program_id(0) >= 0, a[:, 0:LANES], dep)
    return jnp.concatenate([head, a[:, LANES:]], axis=1)


def _swiglu(xb, win_ref, wo_ref, finish):
    rows = xb.shape[0] // FF_ANCHORS
    acc = None
    pending = None
    for k, (c0, cw) in enumerate(FF_CHUNKS):
        gate = _dot(xb, win_ref[:, c0:c0 + cw])
        up = _dot(xb, win_ref[:, D_FF + c0:D_FF + c0 + cw])
        if k == 0:
            done = finish()
        elif k <= FF_ANCHORS:
            up = jnp.concatenate(
                [_after(up[r0:r0 + rows], done[r0:r0 + rows]) if r0 == (k - 1) * rows
                 else up[r0:r0 + rows] for r0 in range(0, xb.shape[0], rows)], axis=0)
        if pending is not None:
            h, rows_out = pending
            part = _dot(h, wo_ref[rows_out, :])
            acc = part if acc is None else acc + part
        pending = ((gate * jax.nn.sigmoid(gate) * up).astype(BF16), slice(c0, c0 + cw))
    h, rows_out = pending
    return acc + _dot(h, wo_ref[rows_out, :])


def _time_blocks(a, tl):
    return [jnp.concatenate([a[b * tl + t0:b * tl + t0 + PERM_TL] for b in range(BATCH)], axis=0)
            for t0 in range(0, tl, PERM_TL)]


def _cast_blocks(src_refs, dst_refs):
    for src, dst in zip(src_refs, dst_refs, strict=True):
        dst[...] = src[...].astype(BF16)


def _two_stage(n_tiles, pre_ref, matmul_stage, finish_stage, first_step=0):
    step = pl.program_id(0) - first_step

    @pl.when(step == 0)
    def _():
        pre_ref[...] = jnp.zeros_like(pre_ref)

    @pl.when((step >= 0) & (step < n_tiles))
    def _():
        pre = matmul_stage(finish_stage)
        if pre is not None:
            pre_ref[...] = pre

    @pl.when(step == n_tiles)
    def _():
        finish_stage()


def _ffn_ln_kernel(alpha, n_cast, n_tiles, x_ref, win_f32_ref, wo_f32_ref, g_ref, b_ref, *rest):
    cast_src, (o_ref, *cast_dst) = rest[:n_cast], rest[n_cast:-3]
    win_ref, wo_ref, pre_ref = rest[-3:]

    step = pl.program_id(0)

    @pl.when(step < W_CAST_STEPS)
    def _():
        for src, dst in ((win_f32_ref, win_ref), (wo_f32_ref, wo_ref)):
            rows = src.shape[0]
            dst[pl.ds(pl.multiple_of(step * rows, rows), rows), :] = src[...].astype(BF16)

    def finish_rows(r):
        out = _layer_norm(pre_ref[r, :], g_ref[...], b_ref[...])
        o_ref[r, :] = out
        return out

    def matmul_stage(_):
        _cast_blocks(cast_src, cast_dst)
        _ffn_sub_tiles(alpha, x_ref, win_ref, wo_ref, pre_ref, finish_rows)

    def finish_stage():
        for r in FFN_SUB_ROWS:
            finish_rows(r)

    _two_stage(n_tiles, pre_ref, matmul_stage, finish_stage, first_step=W_CAST_STEPS)


def _ffn_sub_tiles(alpha, x_ref, win_ref, wo_ref, pre_ref, finish_rows):
    for r in FFN_SUB_ROWS:
        f = _swiglu(x_ref[r, :].astype(BF16), win_ref, wo_ref, functools.partial(finish_rows, r))
        pre_ref[r, :] = alpha * x_ref[r, :] + 0.5 * f


def _ffn_ple_kernel(alpha, n_tiles, x_ref, p_ref, win_ref, wo_ref, g3_ref, b3_ref,
                    wpi_ref, wpg_ref, g4_ref, b4_ref, o_ref, pre_ref):
    def finish_rows(r):
        x3 = _layer_norm(pre_ref[r, :], g3_ref[...], b3_ref[...])
        e = _dot(p_ref[r, :].astype(BF16), wpi_ref[...]) * jax.nn.sigmoid(
            _dot(x3.astype(BF16), wpg_ref[...]))
        out = _layer_norm(alpha * x3 + e, g4_ref[...], b4_ref[...])
        o_ref[r, :] = out
        return out

    def matmul_stage(_):
        _ffn_sub_tiles(alpha, x_ref, win_ref, wo_ref, pre_ref, finish_rows)

    def finish_stage():
        for r in FFN_SUB_ROWS:
            finish_rows(r)

    _two_stage(n_tiles, pre_ref, matmul_stage, finish_stage)


def _mixer_kernel(alpha, n_cast, n_tiles, x_ref, win_ref, cw_ref, cb_ref, cwo_ref, lam_ref,
                  bd_ref, cd_ref, dsk_ref, perm_ref, unperm_ref, wglu_ref, wout_ref, g_ref,
                  b_ref, *rest):
    cast_src, o_ref = rest[:n_cast], rest[n_cast]
    cast_dst = rest[n_cast + 1:2 * n_cast + 1]
    s_ref, sb_ref, st_ref, v_ref, pre_ref = rest[2 * n_cast + 1:]
    tl = x_ref.shape[1]
    tm = BATCH * tl
    step = pl.program_id(0)

    @pl.when(step == 0)
    def _():
        st_ref[...] = jnp.zeros_like(st_ref)
        v_ref[:, 0:CONV_PAD, :] = jnp.zeros((BATCH, CONV_PAD, CONV_CH), F32)

    def finish_stage():
        out = _layer_norm(pre_ref[...], g_ref[...], b_ref[...])
        o_ref[...] = out.reshape(BATCH, tl, D_MODEL)
        return out

    def matmul_stage(finish):
        _cast_blocks(cast_src, cast_dst)
        return _mixer_matmul_stage(
            alpha, tl, x_ref, win_ref, cw_ref, cb_ref, cwo_ref, lam_ref, bd_ref, cd_ref,
            dsk_ref, perm_ref, unperm_ref, wglu_ref, wout_ref, s_ref, sb_ref, st_ref, v_ref,
            finish)

    _two_stage(n_tiles, pre_ref, matmul_stage, finish_stage)


def _mixer_matmul_stage(alpha, tl, x_ref, win_ref, cw_ref, cb_ref, cwo_ref, lam_ref, bd_ref,
                        cd_ref, dsk_ref, perm_ref, unperm_ref, wglu_ref, wout_ref, s_ref,
                        sb_ref, st_ref, v_ref, finish):
    tm = BATCH * tl
    xb = x_ref[...].reshape(tm, D_MODEL).astype(BF16)

    def proj(c0, cw):
        return _dot(xb, win_ref[:, c0:c0 + cw])

    n_blocks = tl // PERM_TL
    cols = [(slice(2 * h * HALF_ST, (2 * h + 1) * HALF_ST),
             slice((2 * h + 1) * HALF_ST, (2 * h + 2) * HALF_ST)) for h in range(2)]
    block_rows = [slice(j * PERM_ROWS, (j + 1) * PERM_ROWS) for j in range(n_blocks)]

    half = D_MODEL // 2
    u_blocks = _time_blocks(proj(3 * CONV_CH, SSM_WIDTH), tl)
    u_blocks = [_after(u, d) for u, d in zip(u_blocks, _time_blocks(finish(), tl), strict=True)]
    gate_conv = [proj(4 * CONV_CH, half)]
    ub_blocks = [_dot(perm_ref[...], u.astype(BF16)).astype(BF16) for u in u_blocks]

    def b_proj(j, h):
        s_ref[block_rows[j], 2 * h * HALF_ST:2 * (h + 1) * HALF_ST] = _dot(
            ub_blocks[j][:, h * HALF_IN:(h + 1) * HALF_IN], bd_ref[h])

    def scan_block(j, state):
        out = []
        for (re, im), (sr, si) in zip(cols, state, strict=True):
            lam_re, lam_im = lam_ref[:, re], lam_ref[:, im]
            for l in range(j * PERM_TL, (j + 1) * PERM_TL, BF16_ROWS // SUBLANES):
                pair_re, pair_im = [], []
                for t in range(l, l + BF16_ROWS // SUBLANES):
                    rows = slice(t * SUBLANES, (t + 1) * SUBLANES)
                    sr, si = (lam_re * sr - lam_im * si + s_ref[rows, re],
                              lam_re * si + lam_im * sr + s_ref[rows, im])
                    pair_re.append(sr)
                    pair_im.append(si)
                rows = slice(l * SUBLANES, l * SUBLANES + BF16_ROWS)
                sb_ref[rows, re] = jnp.concatenate(pair_re, axis=0).astype(BF16)
                sb_ref[rows, im] = jnp.concatenate(pair_im, axis=0).astype(BF16)
            out.append((sr, si))
        return out

    def c_proj(j):
        return jnp.concatenate(
            [_dot(sb_ref[block_rows[j], 2 * h * HALF_ST:2 * (h + 1) * HALF_ST], cd_ref[h])
             for h in range(2)], axis=-1)

    assert n_blocks == 2
    state = [(st_ref[:, re], st_ref[:, im]) for re, im in cols]
    b_proj(0, 0)
    gate_conv.append(proj(4 * CONV_CH + half, half))
    b_proj(0, 1)
    gate_ssm = [proj(4 * CONV_CH + D_MODEL, half)]
    state = scan_block(0, state)
    b_proj(1, 0)
    gate_ssm.append(proj(4 * CONV_CH + D_MODEL + half, half))
    b_proj(1, 1)

    v = proj(CONV_CH, CONV_CH) * proj(2 * CONV_CH, CONV_CH)
    v_ref[:, CONV_PAD:CONV_PAD + tl, :] = v.reshape(BATCH, tl, CONV_CH)
    z = cb_ref[...].reshape(1, 1, CONV_CH)
    for k in range(CONV_K):
        lag = CONV_K - 1 - k
        z = z + cw_ref[k:k + 1, :].reshape(1, 1, CONV_CH) * v_ref[
            :, CONV_PAD - lag:CONV_PAD - lag + tl, :]
    v_ref[:, 0:CONV_PAD, :] = v_ref[:, tl:tl + CONV_PAD, :]
    conv_gated = (proj(0, CONV_CH) * z.reshape(tm, CONV_CH)).astype(BF16)
    y_tb_blocks = [c_proj(0)]
    state = scan_block(1, state)
    y_conv = _dot(conv_gated, cwo_ref[...])
    y_tb_blocks.append(c_proj(1))
    for (re, im), (sr, si) in zip(cols, state, strict=True):
        st_ref[:, re] = sr
        st_ref[:, im] = si

    gate_blocks = _time_blocks(jax.nn.sigmoid(jnp.concatenate(gate_ssm, axis=1)), tl)
    merged_blocks = _time_blocks(jax.nn.sigmoid(jnp.concatenate(gate_conv, axis=1)) * y_conv, tl)
    y_blocks = []
    for y_tb, u, gate in zip(y_tb_blocks, u_blocks, gate_blocks, strict=True):
        y_tb = _after(y_tb, gate)
        hi = y_tb.astype(BF16)
        lo = (y_tb - hi.astype(F32)).astype(BF16)
        y_blocks.append(
            _dot(unperm_ref[...], jnp.concatenate([hi, lo], axis=0)) + dsk_ref[...] * u)
    sg_blocks = [jax.nn.gelu(_after(y, mc)).astype(BF16)
                 for y, mc in zip(y_blocks, merged_blocks, strict=True)]
    glu_a = [_dot(sg, wglu_ref[:, 0:D_MODEL]) for sg in sg_blocks]
    glu_b = [_dot(sg, wglu_ref[:, D_MODEL:2 * D_MODEL]) for sg in sg_blocks]
    pre_blocks = []
    for j in range(n_blocks):
        m = merged_blocks[j] + gate_blocks[j] * (glu_a[j] * jax.nn.sigmoid(glu_b[j]))
        x = x_ref[:, j * PERM_TL:(j + 1) * PERM_TL, :].reshape(PERM_ROWS, D_MODEL)
        pre_blocks.append(alpha * x + _dot(m.astype(BF16), wout_ref[...]))
    return jnp.concatenate([blk[b * PERM_TL:(b + 1) * PERM_TL]
                            for b in range(BATCH) for blk in pre_blocks], axis=0)


def _resident(shape):
    zeros = (0,) * len(shape)
    return pl.BlockSpec(shape, lambda i: zeros, pipeline_mode=pl.Buffered(1))


def _matmul_tile(n_tiles, first_step=0):
    return lambda i: jnp.clip(i - first_step, 0, n_tiles - 1)


def _finish_tile(first_step=0):
    return lambda i: jnp.maximum(i - first_step - 1, 0)


def _rows(tm, width, tile_of_step):
    return pl.BlockSpec((tm, width), lambda i: (tile_of_step(i), 0))


def _cast_plan(weights, n_tiles, first_step=0):
    specs, shapes = [], []
    tile = _matmul_tile(n_tiles, first_step)
    for w in weights:
        rows, cols = w.shape
        n_blocks = n_tiles
        while (rows // n_blocks) % BF16_ROWS or rows % n_blocks:
            n_blocks //= 2
        repeat = n_tiles // n_blocks
        specs.append(pl.BlockSpec((rows // n_blocks, cols),
                                  lambda i, repeat=repeat: (tile(i) // repeat, 0)))
        shapes.append(jax.ShapeDtypeStruct(w.shape, BF16))
    return specs, shapes


def _ffn_ln(x, win, wo, g, b, next_weights, alpha):
    n_tiles = ROWS // TM_FFN
    first = W_CAST_STEPS
    cast_specs, cast_shapes = _cast_plan(next_weights, n_tiles, first)

    def own_weight(w):
        return pl.BlockSpec((w.shape[0] // first, w.shape[1]),
                            lambda i: (jnp.minimum(i, first - 1), 0))

    return pl.pallas_call(
        functools.partial(_ffn_ln_kernel, alpha, len(next_weights), n_tiles),
        grid=(first + n_tiles + 1,),
        in_specs=[_rows(TM_FFN, D_MODEL, _matmul_tile(n_tiles, first)), own_weight(win),
                  own_weight(wo), _resident(g.shape), _resident(b.shape)] + cast_specs,
        out_specs=[_rows(TM_FFN, D_MODEL, _finish_tile(first))] + cast_specs,
        out_shape=[jax.ShapeDtypeStruct((ROWS, D_MODEL), F32)] + cast_shapes,
        scratch_shapes=[pltpu.VMEM(win.shape, BF16), pltpu.VMEM(wo.shape, BF16),
                        pltpu.VMEM((TM_FFN, D_MODEL), F32)],
        compiler_params=pltpu.CompilerParams(
            dimension_semantics=("arbitrary",), vmem_limit_bytes=VMEM_LIMIT_BYTES),
        name="ffn_ln",
    )(x, win, wo, g, b, *next_weights)


def _ffn_ple(x, p, win, wo, g3, b3, wpi, wpg, g4, b4, alpha):
    n_tiles = ROWS // TM_FFN
    return pl.pallas_call(
        functools.partial(_ffn_ple_kernel, alpha, n_tiles),
        grid=(n_tiles + 1,),
        in_specs=[_rows(TM_FFN, D_MODEL, _matmul_tile(n_tiles)),
                  _rows(TM_FFN, PLE_DIM, _finish_tile()), _resident(win.shape),
                  _resident(wo.shape), _resident(g3.shape),
                  _resident(b3.shape), _resident(wpi.shape), _resident(wpg.shape),
                  _resident(g4.shape), _resident(b4.shape)],
        out_specs=_rows(TM_FFN, D_MODEL, _finish_tile()),
        out_shape=jax.ShapeDtypeStruct((ROWS, D_MODEL), F32),
        scratch_shapes=[pltpu.VMEM((TM_FFN, D_MODEL), F32)],
        compiler_params=pltpu.CompilerParams(
            dimension_semantics=("arbitrary",), vmem_limit_bytes=VMEM_LIMIT_BYTES),
        name="ffn_ple",
    )(x, p, win, wo, g3, b3, wpi, wpg, g4, b4)


def _mixer(x, win, cw, cb, cwo, lam, bd, cd, dsk, perm, unperm, wglu, wout, g, b,
           next_weights, alpha):
    consts = (win, cw, cb, cwo, lam, bd, cd, dsk, perm, unperm, wglu, wout, g, b)
    n_tiles = SEQ // TL_MIX
    cast_specs, cast_shapes = _cast_plan(next_weights, n_tiles)
    matmul_tile, finish_tile = _matmul_tile(n_tiles), _finish_tile()
    block = (BATCH, TL_MIX, D_MODEL)
    return pl.pallas_call(
        functools.partial(_mixer_kernel, alpha, len(next_weights), n_tiles),
        grid=(n_tiles + 1,),
        in_specs=[pl.BlockSpec(block, lambda i: (0, matmul_tile(i), 0))]
        + [_resident(c.shape) for c in consts] + cast_specs,
        out_specs=[pl.BlockSpec(block, lambda i: (0, finish_tile(i), 0))] + cast_specs,
        out_shape=[jax.ShapeDtypeStruct((BATCH, SEQ, D_MODEL), F32)] + cast_shapes,
        scratch_shapes=[
            pltpu.VMEM((TM_MIX, 4 * HALF_ST), F32),
            pltpu.VMEM((TM_MIX, 4 * HALF_ST), BF16),
            pltpu.VMEM((SUBLANES, 4 * HALF_ST), F32),
            pltpu.VMEM((BATCH, CONV_PAD + TL_MIX, CONV_CH), F32),
            pltpu.VMEM((TM_MIX, D_MODEL), F32),
        ],
        compiler_params=pltpu.CompilerParams(
            dimension_semantics=("arbitrary",), vmem_limit_bytes=VMEM_LIMIT_BYTES),
        name="mixer",
    )(x, *consts, *next_weights)


def _ssm_matrices(lam_re, lam_im, log_step, b_re, b_im, c_re, c_im):
    dt = jnp.exp(log_step)[:, None]
    mag = jnp.exp(lam_re * dt)
    lbar_re = mag * jnp.cos(lam_im * dt)
    lbar_im = mag * jnp.sin(lam_im * dt)
    den = lam_re * lam_re + lam_im * lam_im
    q_re = ((lbar_re - 1.0) * lam_re + lbar_im * lam_im) / den
    q_im = (lbar_im * lam_re - (lbar_re - 1.0) * lam_im) / den
    bbar_re = q_re[..., None] * b_re - q_im[..., None] * b_im
    bbar_im = q_re[..., None] * b_im + q_im[..., None] * b_re

    n, w = SSM_STATE, SSM_GROUP
    lam = jnp.stack([lbar_re, lbar_im]).reshape(2, 2, HALF_GROUPS, n)
    lam = jnp.broadcast_to(lam.transpose(1, 0, 2, 3).reshape(1, 4 * HALF_ST),
                           (SUBLANES, 4 * HALF_ST))
    tb = jnp.stack([bbar_re, bbar_im], axis=1).reshape(2, HALF_GROUPS, 2, n, w)
    tb = tb.transpose(0, 1, 4, 2, 3).reshape(2, HALF_IN, 2 * n)
    tc = jnp.stack([c_re, -c_im]).reshape(2, 2, HALF_GROUPS, w, n)
    tc = tc.transpose(1, 0, 2, 4, 3).reshape(2, 2 * HALF_ST, w)

    col = np.arange(2 * HALF_ST)
    ch = np.arange(HALF_IN)
    k = np.arange(2 * n)
    spread_b = (k[:, None] // n == col[None, :] // HALF_ST) & (k[:, None] % n == col[None, :] % n)
    own_b = ch[:, None] // w == (col[None, :] % HALF_ST) // n
    spread_c = np.arange(w)[:, None] == ch[None, :] % w
    bd = jnp.einsum("hrk,kc->hrc", tb, jnp.asarray(spread_b, F32)) * jnp.asarray(own_b, F32)
    cd = jnp.einsum("hrk,kc->hrc", tc, jnp.asarray(spread_c, F32)) * jnp.asarray(own_b.T, F32)
    return lam, bd.astype(BF16), cd.astype(BF16)


def _row_permutations():
    dst = np.arange(PERM_ROWS)
    src = (dst % BATCH) * PERM_TL + dst // BATCH
    perm = (src[:, None] == np.arange(PERM_ROWS)[None, :]).astype(np.float32)
    return (jnp.asarray(perm, dtype=BF16),
            jnp.asarray(np.concatenate([perm.T, perm.T], axis=1), dtype=BF16))


def kernel(x, p, ffn1_w_in, ffn1_w_out, ln1_g, ln1_b, mix_w_in, conv_w, conv_b, conv_w_out, ssm_lam_re, ssm_lam_im, ssm_log_step, ssm_b_re, ssm_b_im, ssm_c_re, ssm_c_im, ssm_d, ssm_w_glu, mix_w_out, ln2_g, ln2_b, ffn2_w_in, ffn2_w_out, ln3_g, ln3_b, ple_w_in, ple_w_gate, ln4_g, ln4_b):
    depth = p.shape[0]
    alpha = (2.0 * depth) ** 0.25
    assert x.shape == (BATCH, SEQ, D_MODEL) and BATCH == SUBLANES

    def row(v, i):
        return v[i:i + 1].reshape(1, -1)

    perm, unperm = _row_permutations()
    h = x.reshape(ROWS, D_MODEL)
    for i in range(depth):
        h, mw_in, cw_out, w_glu, mw_out = _ffn_ln(
            h, ffn1_w_in[i], ffn1_w_out[i], row(ln1_g, i),
            row(ln1_b, i), (mix_w_in[i], conv_w_out[i], ssm_w_glu[i], mix_w_out[i]), alpha)
        lam, bd, cd = _ssm_matrices(ssm_lam_re[i], ssm_lam_im[i], ssm_log_step[i],
                                    ssm_b_re[i], ssm_b_im[i], ssm_c_re[i], ssm_c_im[i])
        h, f2_in, f2_out, pw_in, pw_gate = _mixer(
            h.reshape(BATCH, SEQ, D_MODEL), mw_in, conv_w[i], row(conv_b, i), cw_out, lam,
            bd, cd, row(ssm_d, i), perm, unperm, w_glu, mw_out, row(ln2_g, i), row(ln2_b, i),
            (ffn2_w_in[i], ffn2_w_out[i], ple_w_in[i], ple_w_gate[i]), alpha)
        h = _ffn_ple(h.reshape(ROWS, D_MODEL), p[i].reshape(ROWS, PLE_DIM), f2_in, f2_out,
                     row(ln3_g, i), row(ln3_b, i), pw_in, pw_gate,
                     row(ln4_g, i), row(ln4_b, i), alpha)
    return h.reshape(BATCH, SEQ, D_MODEL)
```

```python
import functools

import jax
import jax.numpy as jnp
import numpy as np
from jax import lax
from jax.experimental import pallas as pl
from jax.experimental.pallas import tpu as pltpu

D_MODEL = 1024
BATCH = 8
SEQ = 2048
PLE_DIM = 256
D_FF = 2816
CONV_CH = 512
CONV_K = 3
SSM_WIDTH = 512
SSM_GROUP = 16
SSM_GROUPS = SSM_WIDTH // SSM_GROUP
SSM_STATE = 64
LN_EPS = 1e-5

SUBLANES = 8
LANES = 128
BF16_ROWS = 16
VMEM_LIMIT_BYTES = 56 * 1024 * 1024

ROWS = BATCH * SEQ
TM_FFN = 1024
FFN_SUB_ROWS = tuple(slice(r, r + 512) for r in range(0, TM_FFN, 512))
TL_MIX = 64
TM_MIX = BATCH * TL_MIX
FF_CHUNKS = ((0, 768), (768, 768), (1536, 768), (2304, 512))
FF_ANCHORS = 2
W_CAST_STEPS = 8
HALF_GROUPS = SSM_GROUPS // 2
HALF_IN = HALF_GROUPS * SSM_GROUP
HALF_ST = HALF_GROUPS * SSM_STATE
CONV_PAD = SUBLANES
MXU_DIM = 256
PERM_ROWS = MXU_DIM
PERM_TL = PERM_ROWS // BATCH

BF16 = jnp.bfloat16
F32 = jnp.float32


def _dot(a, b):
    return jnp.dot(a, b, preferred_element_type=F32)


def _layer_norm(y, g, b):
    mu = jnp.mean(y, axis=-1, keepdims=True)
    yc = y - mu
    var = jnp.mean(yc * yc, axis=-1, keepdims=True)
    return yc * lax.rsqrt(var + LN_EPS) * g + b


def _after(a, done):
    assert a.shape[0] == done.shape[0]
    dep = done[:, 0:LANES]
    for c in range(LANES, done.shape[1], LANES):
        dep = jnp.maximum(dep, done[:, c:c + LANES])
    head = jnp.where(pl.program_id(0) >= 0, a[:, 0:LANES], dep)
    return jnp.concatenate([head, a[:, LANES:]], axis=1)


def _swiglu(xb, win_ref, wo_ref, finish):
    rows = xb.shape[0] // FF_ANCHORS
    acc = None
    pending = None
    for k, (c0, cw) in enumerate(FF_CHUNKS):
        gate = _dot(xb, win_ref[:, c0:c0 + cw])
        up = _dot(xb, win_ref[:, D_FF + c0:D_FF + c0 + cw])
        if k == 0:
            done = finish()
        elif k <= FF_ANCHORS:
            up = jnp.concatenate(
                [_after(up[r0:r0 + rows], done[r0:r0 + rows]) if r0 == (k - 1) * rows
                 else up[r0:r0 + rows] for r0 in range(0, xb.shape[0], rows)], axis=0)
        if pending is not None:
            h, rows_out = pending
            part = _dot(h, wo_ref[rows_out, :])
            acc = part if acc is None else acc + part
        pending = ((gate * jax.nn.sigmoid(gate) * up).astype(BF16), slice(c0, c0 + cw))
    h, rows_out = pending
    return acc + _dot(h, wo_ref[rows_out, :])


def _time_blocks(a, tl):
    return [jnp.concatenate([a[b * tl + t0:b * tl + t0 + PERM_TL] for b in range(BATCH)], axis=0)
            for t0 in range(0, tl, PERM_TL)]


def _cast_blocks(src_refs, dst_refs):
    for src, dst in zip(src_refs, dst_refs, strict=True):
        dst[...] = src[...].astype(BF16)


def _two_stage(n_tiles, pre_ref, matmul_stage, finish_stage, first_step=0):
    step = pl.program_id(0) - first_step

    @pl.when(step == 0)
    def _():
        pre_ref[...] = jnp.zeros_like(pre_ref)

    @pl.when((step >= 0) & (step < n_tiles))
    def _():
        pre = matmul_stage(finish_stage)
        if pre is not None:
            pre_ref[...] = pre

    @pl.when(step == n_tiles)
    def _():
        finish_stage()


def _ffn_ln_kernel(alpha, n_cast, n_tiles, x_ref, win_f32_ref, wo_f32_ref, g_ref, b_ref, *rest):
    cast_src, (o_ref, *cast_dst) = rest[:n_cast], rest[n_cast:-3]
    win_ref, wo_ref, pre_ref = rest[-3:]

    step = pl.program_id(0)

    @pl.when(step < W_CAST_STEPS)
    def _():
        for src, dst in ((win_f32_ref, win_ref), (wo_f32_ref, wo_ref)):
            rows = src.shape[0]
            dst[pl.ds(pl.multiple_of(step * rows, rows), rows), :] = src[...].astype(BF16)

    def finish_rows(r):
        out = _layer_norm(pre_ref[r, :], g_ref[...], b_ref[...])
        o_ref[r, :] = out
        return out

    def matmul_stage(_):
        _cast_blocks(cast_src, cast_dst)
        _ffn_sub_tiles(alpha, x_ref, win_ref, wo_ref, pre_ref, finish_rows)

    def finish_stage():
        for r in FFN_SUB_ROWS:
            finish_rows(r)

    _two_stage(n_tiles, pre_ref, matmul_stage, finish_stage, first_step=W_CAST_STEPS)


def _ffn_sub_tiles(alpha, x_ref, win_ref, wo_ref, pre_ref, finish_rows):
    for r in FFN_SUB_ROWS:
        f = _swiglu(x_ref[r, :].astype(BF16), win_ref, wo_ref, functools.partial(finish_rows, r))
        pre_ref[r, :] = alpha * x_ref[r, :] + 0.5 * f


def _ffn_ple_kernel(alpha, n_tiles, x_ref, p_ref, win_ref, wo_ref, g3_ref, b3_ref,
                    wpi_ref, wpg_ref, g4_ref, b4_ref, o_ref, pre_ref):
    def finish_rows(r):
        x3 = _layer_norm(pre_ref[r, :], g3_ref[...], b3_ref[...])
        e = _dot(p_ref[r, :].astype(BF16), wpi_ref[...]) * jax.nn.sigmoid(
            _dot(x3.astype(BF16), wpg_ref[...]))
        out = _layer_norm(alpha * x3 + e, g4_ref[...], b4_ref[...])
        o_ref[r, :] = out
        return out

    def matmul_stage(_):
        _ffn_sub_tiles(alpha, x_ref, win_ref, wo_ref, pre_ref, finish_rows)

    def finish_stage():
        for r in FFN_SUB_ROWS:
            finish_rows(r)

    _two_stage(n_tiles, pre_ref, matmul_stage, finish_stage)


def _mixer_kernel(alpha, n_cast, n_tiles, x_ref, win_ref, cw_ref, cb_ref, cwo_ref, lam_ref,
                  bd_ref, cd_ref, dsk_ref, perm_ref, unperm_ref, wglu_ref, wout_ref, g_ref,
                  b_ref, *rest):
    cast_src, o_ref = rest[:n_cast], rest[n_cast]
    cast_dst = rest[n_cast + 1:2 * n_cast + 1]
    s_ref, sb_ref, st_ref, v_ref, pre_ref = rest[2 * n_cast + 1:]
    tl = x_ref.shape[1]
    tm = BATCH * tl
    step = pl.program_id(0)

    @pl.when(step == 0)
    def _():
        st_ref[...] = jnp.zeros_like(st_ref)
        v_ref[:, 0:CONV_PAD, :] = jnp.zeros((BATCH, CONV_PAD, CONV_CH), F32)

    def finish_stage():
        out = _layer_norm(pre_ref[...], g_ref[...], b_ref[...])
        o_ref[...] = out.reshape(BATCH, tl, D_MODEL)
        return out

    def matmul_stage(finish):
        _cast_blocks(cast_src, cast_dst)
        return _mixer_matmul_stage(
            alpha, tl, x_ref, win_ref, cw_ref, cb_ref, cwo_ref, lam_ref, bd_ref, cd_ref,
            dsk_ref, perm_ref, unperm_ref, wglu_ref, wout_ref, s_ref, sb_ref, st_ref, v_ref,
            finish)

    _two_stage(n_tiles, pre_ref, matmul_stage, finish_stage)


def _mixer_matmul_stage(alpha, tl, x_ref, win_ref, cw_ref, cb_ref, cwo_ref, lam_ref, bd_ref,
                        cd_ref, dsk_ref, perm_ref, unperm_ref, wglu_ref, wout_ref, s_ref,
                        sb_ref, st_ref, v_ref, finish):
    tm = BATCH * tl
    xb = x_ref[...].reshape(tm, D_MODEL).astype(BF16)

    def proj(c0, cw):
        return _dot(xb, win_ref[:, c0:c0 + cw])

    n_blocks = tl // PERM_TL
    cols = [(slice(2 * h * HALF_ST, (2 * h + 1) * HALF_ST),
             slice((2 * h + 1) * HALF_ST, (2 * h + 2) * HALF_ST)) for h in range(2)]
    block_rows = [slice(j * PERM_ROWS, (j + 1) * PERM_ROWS) for j in range(n_blocks)]

    half = D_MODEL // 2
    u_blocks = _time_blocks(proj(3 * CONV_CH, SSM_WIDTH), tl)
    done_blocks = _time_blocks(finish(), tl)
    u_blocks[0] = _after(u_blocks[0], done_blocks[0])
    gate_conv = [proj(4 * CONV_CH, half)]
    ub_f32 = [_dot(perm_ref[...], u.astype(BF16)) for u in u_blocks]
    ub_f32[1] = _after(ub_f32[1], done_blocks[1])
    ub_blocks = [ub.astype(BF16) for ub in ub_f32]

    def b_proj(j, h):
        s_ref[block_rows[j], 2 * h * HALF_ST:2 * (h + 1) * HALF_ST] = _dot(
            ub_blocks[j][:, h * HALF_IN:(h + 1) * HALF_IN], bd_ref[h])

    def scan_block(j, state):
        out = []
        for (re, im), (sr, si) in zip(cols, state, strict=True):
            lam_re, lam_im = lam_ref[:, re], lam_ref[:, im]
            for l in range(j * PERM_TL, (j + 1) * PERM_TL, BF16_ROWS // SUBLANES):
                pair_re, pair_im = [], []
                for t in range(l, l + BF16_ROWS // SUBLANES):
                    rows = slice(t * SUBLANES, (t + 1) * SUBLANES)
                    sr, si = (lam_re * sr - lam_im * si + s_ref[rows, re],
                              lam_re * si + lam_im * sr + s_ref[rows, im])
                    pair_re.append(sr)
                    pair_im.append(si)
                rows = slice(l * SUBLANES, l * SUBLANES + BF16_ROWS)
                sb_ref[rows, re] = jnp.concatenate(pair_re, axis=0).astype(BF16)
                sb_ref[rows, im] = jnp.concatenate(pair_im, axis=0).astype(BF16)
            out.append((sr, si))
        return out

    def c_proj(j):
        return jnp.concatenate(
            [_dot(sb_ref[block_rows[j], 2 * h * HALF_ST:2 * (h + 1) * HALF_ST], cd_ref[h])
             for h in range(2)], axis=-1)

    assert n_blocks == 2
    state = [(st_ref[:, re], st_ref[:, im]) for re, im in cols]
    b_proj(0, 0)
    gate_conv.append(proj(4 * CONV_CH + half, half))
    b_proj(0, 1)
    gate_ssm = [proj(4 * CONV_CH + D_MODEL, half)]
    state = scan_block(0, state)
    b_proj(1, 0)
    gate_ssm.append(proj(4 * CONV_CH + D_MODEL + half, half))
    b_proj(1, 1)

    v = proj(CONV_CH, CONV_CH) * proj(2 * CONV_CH, CONV_CH)
    v_ref[:, CONV_PAD:CONV_PAD + tl, :] = v.reshape(BATCH, tl, CONV_CH)
    z = cb_ref[...].reshape(1, 1, CONV_CH)
    for k in range(CONV_K):
        lag = CONV_K - 1 - k
        z = z + cw_ref[k:k + 1, :].reshape(1, 1, CONV_CH) * v_ref[
            :, CONV_PAD - lag:CONV_PAD - lag + tl, :]
    v_ref[:, 0:CONV_PAD, :] = v_ref[:, tl:tl + CONV_PAD, :]
    conv_gated = (proj(0, CONV_CH) * z.reshape(tm, CONV_CH)).astype(BF16)
    y_tb_blocks = [c_proj(0)]
    state = scan_block(1, state)
    y_conv = _dot(conv_gated, cwo_ref[...])
    y_tb_blocks.append(c_proj(1))
    for (re, im), (sr, si) in zip(cols, state, strict=True):
        st_ref[:, re] = sr
        st_ref[:, im] = si

    gate_blocks = _time_blocks(jax.nn.sigmoid(jnp.concatenate(gate_ssm, axis=1)), tl)
    merged_blocks = _time_blocks(jax.nn.sigmoid(jnp.concatenate(gate_conv, axis=1)) * y_conv, tl)
    y_blocks = []
    for y_tb, u, gate in zip(y_tb_blocks, u_blocks, gate_blocks, strict=True):
        y_tb = _after(y_tb, gate)
        hi = y_tb.astype(BF16)
        lo = (y_tb - hi.astype(F32)).astype(BF16)
        y_blocks.append(
            _dot(unperm_ref[...], jnp.concatenate([hi, lo], axis=0)) + dsk_ref[...] * u)
    sg_blocks = [jax.nn.gelu(_after(y, mc)).astype(BF16)
                 for y, mc in zip(y_blocks, merged_blocks, strict=True)]
    glu_a = [_dot(sg, wglu_ref[:, 0:D_MODEL]) for sg in sg_blocks]
    glu_b = [_dot(sg, wglu_ref[:, D_MODEL:2 * D_MODEL]) for sg in sg_blocks]
    pre_blocks = []
    for j in range(n_blocks):
        m = merged_blocks[j] + gate_blocks[j] * (glu_a[j] * jax.nn.sigmoid(glu_b[j]))
        x = x_ref[:, j * PERM_TL:(j + 1) * PERM_TL, :].reshape(PERM_ROWS, D_MODEL)
        pre_blocks.append(alpha * x + _dot(m.astype(BF16), wout_ref[...]))
    return jnp.concatenate([blk[b * PERM_TL:(b + 1) * PERM_TL]
                            for b in range(BATCH) for blk in pre_blocks], axis=0)


def _resident(shape):
    zeros = (0,) * len(shape)
    return pl.BlockSpec(shape, lambda i: zeros, pipeline_mode=pl.Buffered(1))


def _matmul_tile(n_tiles, first_step=0):
    return lambda i: jnp.clip(i - first_step, 0, n_tiles - 1)


def _finish_tile(first_step=0):
    return lambda i: jnp.maximum(i - first_step - 1, 0)


def _rows(tm, width, tile_of_step):
    return pl.BlockSpec((tm, width), lambda i: (tile_of_step(i), 0))


def _cast_plan(weights, n_tiles, first_step=0):
    specs, shapes = [], []
    tile = _matmul_tile(n_tiles, first_step)
    for w in weights:
        rows, cols = w.shape
        n_blocks = n_tiles
        while (rows // n_blocks) % BF16_ROWS or rows % n_blocks:
            n_blocks //= 2
        repeat = n_tiles // n_blocks
        specs.append(pl.BlockSpec((rows // n_blocks, cols),
                                  lambda i, repeat=repeat: (tile(i) // repeat, 0)))
        shapes.append(jax.ShapeDtypeStruct(w.shape, BF16))
    return specs, shapes


def _ffn_ln(x, win, wo, g, b, next_weights, alpha):
    n_tiles = ROWS // TM_FFN
    first = W_CAST_STEPS
    cast_specs, cast_shapes = _cast_plan(next_weights, n_tiles, first)

    def own_weight(w):
        return pl.BlockSpec((w.shape[0] // first, w.shape[1]),
                            lambda i: (jnp.minimum(i, first - 1), 0))

    return pl.pallas_call(
        functools.partial(_ffn_ln_kernel, alpha, len(next_weights), n_tiles),
        grid=(first + n_tiles + 1,),
        in_specs=[_rows(TM_FFN, D_MODEL, _matmul_tile(n_tiles, first)), own_weight(win),
                  own_weight(wo), _resident(g.shape), _resident(b.shape)] + cast_specs,
        out_specs=[_rows(TM_FFN, D_MODEL, _finish_tile(first))] + cast_specs,
        out_shape=[jax.ShapeDtypeStruct((ROWS, D_MODEL), F32)] + cast_shapes,
        scratch_shapes=[pltpu.VMEM(win.shape, BF16), pltpu.VMEM(wo.shape, BF16),
                        pltpu.VMEM((TM_FFN, D_MODEL), F32)],
        compiler_params=pltpu.CompilerParams(
            dimension_semantics=("arbitrary",), vmem_limit_bytes=VMEM_LIMIT_BYTES),
        name="ffn_ln",
    )(x, win, wo, g, b, *next_weights)


def _ffn_ple(x, p, win, wo, g3, b3, wpi, wpg, g4, b4, alpha):
    n_tiles = ROWS // TM_FFN
    return pl.pallas_call(
        functools.partial(_ffn_ple_kernel, alpha, n_tiles),
        grid=(n_tiles + 1,),
        in_specs=[_rows(TM_FFN, D_MODEL, _matmul_tile(n_tiles)),
                  _rows(TM_FFN, PLE_DIM, _finish_tile()), _resident(win.shape),
                  _resident(wo.shape), _resident(g3.shape),
                  _resident(b3.shape), _resident(wpi.shape), _resident(wpg.shape),
                  _resident(g4.shape), _resident(b4.shape)],
        out_specs=_rows(TM_FFN, D_MODEL, _finish_tile()),
        out_shape=jax.ShapeDtypeStruct((ROWS, D_MODEL), F32),
        scratch_shapes=[pltpu.VMEM((TM_FFN, D_MODEL), F32)],
        compiler_params=pltpu.CompilerParams(
            dimension_semantics=("arbitrary",), vmem_limit_bytes=VMEM_LIMIT_BYTES),
        name="ffn_ple",
    )(x, p, win, wo, g3, b3, wpi, wpg, g4, b4)


def _mixer(x, win, cw, cb, cwo, lam, bd, cd, dsk, perm, unperm, wglu, wout, g, b,
           next_weights, alpha):
    consts = (win, cw, cb, cwo, lam, bd, cd, dsk, perm, unperm, wglu, wout, g, b)
    n_tiles = SEQ // TL_MIX
    cast_specs, cast_shapes = _cast_plan(next_weights, n_tiles)
    matmul_tile, finish_tile = _matmul_tile(n_tiles), _finish_tile()
    block = (BATCH, TL_MIX, D_MODEL)
    return pl.pallas_call(
        functools.partial(_mixer_kernel, alpha, len(next_weights), n_tiles),
        grid=(n_tiles + 1,),
        in_specs=[pl.BlockSpec(block, lambda i: (0, matmul_tile(i), 0))]
        + [_resident(c.shape) for c in consts] + cast_specs,
        out_specs=[pl.BlockSpec(block, lambda i: (0, finish_tile(i), 0))] + cast_specs,
        out_shape=[jax.ShapeDtypeStruct((BATCH, SEQ, D_MODEL), F32)] + cast_shapes,
        scratch_shapes=[
            pltpu.VMEM((TM_MIX, 4 * HALF_ST), F32),
            pltpu.VMEM((TM_MIX, 4 * HALF_ST), BF16),
            pltpu.VMEM((SUBLANES, 4 * HALF_ST), F32),
            pltpu.VMEM((BATCH, CONV_PAD + TL_MIX, CONV_CH), F32),
            pltpu.VMEM((TM_MIX, D_MODEL), F32),
        ],
        compiler_params=pltpu.CompilerParams(
            dimension_semantics=("arbitrary",), vmem_limit_bytes=VMEM_LIMIT_BYTES),
        name="mixer",
    )(x, *consts, *next_weights)


def _ssm_matrices(lam_re, lam_im, log_step, b_re, b_im, c_re, c_im):
    dt = jnp.exp(log_step)[:, None]
    mag = jnp.exp(lam_re * dt)
    lbar_re = mag * jnp.cos(lam_im * dt)
    lbar_im = mag * jnp.sin(lam_im * dt)
    den = lam_re * lam_re + lam_im * lam_im
    q_re = ((lbar_re - 1.0) * lam_re + lbar_im * lam_im) / den
    q_im = (lbar_im * lam_re - (lbar_re - 1.0) * lam_im) / den
    bbar_re = q_re[..., None] * b_re - q_im[..., None] * b_im
    bbar_im = q_re[..., None] * b_im + q_im[..., None] * b_re

    n, w = SSM_STATE, SSM_GROUP
    lam = jnp.stack([lbar_re, lbar_im]).reshape(2, 2, HALF_GROUPS, n)
    lam = jnp.broadcast_to(lam.transpose(1, 0, 2, 3).reshape(1, 4 * HALF_ST),
                           (SUBLANES, 4 * HALF_ST))
    tb = jnp.stack([bbar_re, bbar_im], axis=1).reshape(2, HALF_GROUPS, 2, n, w)
    tb = tb.transpose(0, 1, 4, 2, 3).reshape(2, HALF_IN, 2 * n)
    tc = jnp.stack([c_re, c_im]).reshape(2, 2, HALF_GROUPS, w, n)
    tc = tc.transpose(1, 0, 2, 4, 3).reshape(2, 2 * HALF_ST, w)

    col = np.arange(2 * HALF_ST)
    ch = np.arange(HALF_IN)
    k = np.arange(2 * n)
    spread_b = (k[:, None] // n == col[None, :] // HALF_ST) & (k[:, None] % n == col[None, :] % n)
    own_b = ch[:, None] // w == (col[None, :] % HALF_ST) // n
    spread_c = np.arange(w)[:, None] == ch[None, :] % w
    bd = jnp.einsum("hrk,kc->hrc", tb, jnp.asarray(spread_b, F32)) * jnp.asarray(own_b, F32)
    own_c = own_b.T * np.where(col[:, None] < HALF_ST, 1.0, -1.0)
    cd = jnp.einsum("hrk,kc->hrc", tc, jnp.asarray(spread_c, F32)) * jnp.asarray(own_c, F32)
    return lam, bd.astype(BF16), cd.astype(BF16)


def _row_permutations():
    dst = np.arange(PERM_ROWS)
    src = (dst % BATCH) * PERM_TL + dst // BATCH
    perm = (src[:, None] == np.arange(PERM_ROWS)[None, :]).astype(np.float32)
    return (jnp.asarray(perm, dtype=BF16),
            jnp.asarray(np.concatenate([perm.T, perm.T], axis=1), dtype=BF16))


def kernel(x, p, ffn1_w_in, ffn1_w_out, ln1_g, ln1_b, mix_w_in, conv_w, conv_b, conv_w_out, ssm_lam_re, ssm_lam_im, ssm_log_step, ssm_b_re, ssm_b_im, ssm_c_re, ssm_c_im, ssm_d, ssm_w_glu, mix_w_out, ln2_g, ln2_b, ffn2_w_in, ffn2_w_out, ln3_g, ln3_b, ple_w_in, ple_w_gate, ln4_g, ln4_b):
    depth = p.shape[0]
    alpha = (2.0 * depth) ** 0.25
    assert x.shape == (BATCH, SEQ, D_MODEL) and BATCH == SUBLANES

    def row(v, i):
        return v[i:i + 1].reshape(1, -1)

    perm, unperm = _row_permutations()
    h = x.reshape(ROWS, D_MODEL)
    for i in range(depth):
        h, mw_in, cw_out, w_glu, mw_out = _ffn_ln(
            h, ffn1_w_in[i], ffn1_w_out[i], row(ln1_g, i),
            row(ln1_b, i), (mix_w_in[i], conv_w_out[i], ssm_w_glu[i], mix_w_out[i]), alpha)
        lam, bd, cd = _ssm_matrices(ssm_lam_re[i], ssm_lam_im[i], ssm_log_step[i],
                                    ssm_b_re[i], ssm_b_im[i], ssm_c_re[i], ssm_c_im[i])
        h, f2_in, f2_out, pw_in, pw_gate = _mixer(
            h.reshape(BATCH, SEQ, D_MODEL), mw_in, conv_w[i], row(conv_b, i), cw_out, lam,
            bd, cd, row(ssm_d, i), perm, unperm, w_glu, mw_out, row(ln2_g, i), row(ln2_b, i),
            (ffn2_w_in[i], ffn2_w_out[i], ple_w_in[i], ple_w_gate[i]), alpha)
        h = _ffn_ple(h.reshape(ROWS, D_MODEL), p[i].reshape(ROWS, PLE_DIM), f2_in, f2_out,
                     row(ln3_g, i), row(ln3_b, i), pw_in, pw_gate,
                     row(ln4_g, i), row(ln4_b, i), alpha)
    return h.reshape(BATCH, SEQ, D_MODEL)
```

```python
import functools

import jax
import jax.numpy as jnp
import numpy as np
from jax import lax
from jax.experimental import pallas as pl
from jax.experimental.pallas import tpu as pltpu

D_MODEL = 1024
BATCH = 8
SEQ = 2048
PLE_DIM = 256
D_FF = 2816
CONV_CH = 512
CONV_K = 3
SSM_WIDTH = 512
SSM_GROUP = 16
SSM_GROUPS = SSM_WIDTH // SSM_GROUP
SSM_STATE = 64
LN_EPS = 1e-5

SUBLANES = 8
LANES = 128
BF16_ROWS = 16
VMEM_LIMIT_BYTES = 56 * 1024 * 1024

ROWS = BATCH * SEQ
TM_FFN = 1024
FFN_SUB_ROWS = tuple(slice(r, r + 512) for r in range(0, TM_FFN, 512))
TL_MIX = 64
TM_MIX = BATCH * TL_MIX
FF_CHUNKS = ((0, 768), (768, 768), (1536, 768), (2304, 512))
FF_ANCHORS = 2
W_CAST_STEPS = 8
HALF_GROUPS = SSM_GROUPS // 2
HALF_IN = HALF_GROUPS * SSM_GROUP
HALF_ST = HALF_GROUPS * SSM_STATE
CONV_PAD = SUBLANES
MXU_DIM = 256
PERM_ROWS = MXU_DIM
PERM_TL = PERM_ROWS // BATCH

BF16 = jnp.bfloat16
F32 = jnp.float32


def _dot(a, b):
    return jnp.dot(a, b, preferred_element_type=F32)


def _layer_norm(y, g, b):
    mu = jnp.mean(y, axis=-1, keepdims=True)
    yc = y - mu
    var = jnp.mean(yc * yc, axis=-1, keepdims=True)
    return yc * lax.rsqrt(var + LN_EPS) * g + b


def _after(a, done):
    assert a.shape[0] == done.shape[0]
    dep = done[:, 0:LANES]
    for c in range(LANES, done.shape[1], LANES):
        dep = jnp.maximum(dep, done[:, c:c + LANES])
    head = jnp.where(pl.program_id(0) >= 0, a[:, 0:LANES], dep)
    return jnp.concatenate([head, a[:, LANES:]], axis=1)


def _swiglu(xb, win_ref, wo_ref, finish):
    rows = xb.shape[0] // FF_ANCHORS
    acc = None
    pending = None
    for k, (c0, cw) in enumerate(FF_CHUNKS):
        gate = _dot(xb, win_ref[:, c0:c0 + cw])
        up = _dot(xb, win_ref[:, D_FF + c0:D_FF + c0 + cw])
        if k == 0:
            done = finish()
        elif k <= FF_ANCHORS:
            up = jnp.concatenate(
                [_after(up[r0:r0 + rows], done[r0:r0 + rows]) if r0 == (k - 1) * rows
                 else up[r0:r0 + rows] for r0 in range(0, xb.shape[0], rows)], axis=0)
        if pending is not None:
            h, rows_out = pending
            part = _dot(h, wo_ref[rows_out, :])
            acc = part if acc is None else acc + part
        pending = ((gate * jax.nn.sigmoid(gate) * up).astype(BF16), slice(c0, c0 + cw))
    h, rows_out = pending
    return acc + _dot(h, wo_ref[rows_out, :])


def _time_blocks(a, tl):
    return [jnp.concatenate([a[b * tl + t0:b * tl + t0 + PERM_TL] for b in range(BATCH)], axis=0)
            for t0 in range(0, tl, PERM_TL)]


def _cast_blocks(src_refs, dst_refs):
    for src, dst in zip(src_refs, dst_refs, strict=True):
        dst[...] = src[...].astype(BF16)


def _two_stage(n_tiles, pre_ref, matmul_stage, finish_stage, first_step=0):
    step = pl.program_id(0) - first_step

    @pl.when(step == 0)
    def _():
        pre_ref[...] = jnp.zeros_like(pre_ref)

    @pl.when((step >= 0) & (step < n_tiles))
    def _():
        pre = matmul_stage(finish_stage)
        if pre is not None:
            pre_ref[...] = pre

    @pl.when(step == n_tiles)
    def _():
        finish_stage()


def _ffn_ln_kernel(alpha, n_cast, n_tiles, x_ref, win_f32_ref, wo_f32_ref, g_ref, b_ref, *rest):
    cast_src, (o_ref, *cast_dst) = rest[:n_cast], rest[n_cast:-3]
    win_ref, wo_ref, pre_ref = rest[-3:]

    step = pl.program_id(0)

    @pl.when(step < W_CAST_STEPS)
    def _():
        for src, dst in ((win_f32_ref, win_ref), (wo_f32_ref, wo_ref)):
            rows = src.shape[0]
            dst[pl.ds(pl.multiple_of(step * rows, rows), rows), :] = src[...].astype(BF16)

    def finish_rows(r):
        out = _layer_norm(pre_ref[r, :], g_ref[...], b_ref[...])
        o_ref[r, :] = out
        return out

    def matmul_stage(_):
        _cast_blocks(cast_src, cast_dst)
        _ffn_sub_tiles(alpha, x_ref, win_ref, wo_ref, pre_ref, finish_rows)

    def finish_stage():
        for r in FFN_SUB_ROWS:
            finish_rows(r)

    _two_stage(n_tiles, pre_ref, matmul_stage, finish_stage, first_step=W_CAST_STEPS)


def _ffn_sub_tiles(alpha, x_ref, win_ref, wo_ref, pre_ref, finish_rows):
    for r in FFN_SUB_ROWS:
        f = _swiglu(x_ref[r, :].astype(BF16), win_ref, wo_ref, functools.partial(finish_rows, r))
        pre_ref[r, :] = alpha * x_ref[r, :] + 0.5 * f


def _ffn_ple_kernel(alpha, n_tiles, x_ref, p_ref, win_ref, wo_ref, g3_ref, b3_ref,
                    wpi_ref, wpg_ref, g4_ref, b4_ref, o_ref, pre_ref):
    def finish_rows(r):
        x3 = _layer_norm(pre_ref[r, :], g3_ref[...], b3_ref[...])
        e = _dot(p_ref[r, :].astype(BF16), wpi_ref[...]) * jax.nn.sigmoid(
            _dot(x3.astype(BF16), wpg_ref[...]))
        out = _layer_norm(alpha * x3 + e, g4_ref[...], b4_ref[...])
        o_ref[r, :] = out
        return out

    def matmul_stage(_):
        _ffn_sub_tiles(alpha, x_ref, win_ref, wo_ref, pre_ref, finish_rows)

    def finish_stage():
        for r in FFN_SUB_ROWS:
            finish_rows(r)

    _two_stage(n_tiles, pre_ref, matmul_stage, finish_stage)


def _mixer_kernel(alpha, n_cast, n_tiles, x_ref, win_ref, cw_ref, cb_ref, cwo_ref, lam_ref,
                  bd_ref, cd_ref, dsk_ref, perm_ref, unperm_ref, wglu_ref, wout_ref, g_ref,
                  b_ref, *rest):
    cast_src, o_ref = rest[:n_cast], rest[n_cast]
    cast_dst = rest[n_cast + 1:2 * n_cast + 1]
    s_ref, sb_ref, st_ref, v_ref, pre_ref = rest[2 * n_cast + 1:]
    tl = x_ref.shape[1]
    tm = BATCH * tl
    step = pl.program_id(0)

    @pl.when(step == 0)
    def _():
        st_ref[...] = jnp.zeros_like(st_ref)
        v_ref[:, 0:CONV_PAD, :] = jnp.zeros((BATCH, CONV_PAD, CONV_CH), F32)

    def finish_stage():
        out = _layer_norm(pre_ref[...], g_ref[...], b_ref[...])
        o_ref[...] = out.reshape(BATCH, tl, D_MODEL)
        return out

    def matmul_stage(finish):
        _cast_blocks(cast_src, cast_dst)
        return _mixer_matmul_stage(
            alpha, tl, x_ref, win_ref, cw_ref, cb_ref, cwo_ref, lam_ref, bd_ref, cd_ref,
            dsk_ref, perm_ref, unperm_ref, wglu_ref, wout_ref, s_ref, sb_ref, st_ref, v_ref,
            finish)

    _two_stage(n_tiles, pre_ref, matmul_stage, finish_stage)


def _mixer_matmul_stage(alpha, tl, x_ref, win_ref, cw_ref, cb_ref, cwo_ref, lam_ref, bd_ref,
                        cd_ref, dsk_ref, perm_ref, unperm_ref, wglu_ref, wout_ref, s_ref,
                        sb_ref, st_ref, v_ref, finish):
    tm = BATCH * tl
    xb = x_ref[...].reshape(tm, D_MODEL).astype(BF16)

    def proj(c0, cw):
        return _dot(xb, win_ref[:, c0:c0 + cw])

    n_blocks = tl // PERM_TL
    cols = [(slice(2 * h * HALF_ST, (2 * h + 1) * HALF_ST),
             slice((2 * h + 1) * HALF_ST, (2 * h + 2) * HALF_ST)) for h in range(2)]
    block_rows = [slice(j * PERM_ROWS, (j + 1) * PERM_ROWS) for j in range(n_blocks)]

    half = D_MODEL // 2
    u_blocks = _time_blocks(proj(3 * CONV_CH, SSM_WIDTH), tl)
    done_blocks = _time_blocks(finish(), tl)
    u_blocks[0] = _after(u_blocks[0], done_blocks[0])
    gate_conv = [proj(4 * CONV_CH, half)]
    ub_f32 = [_dot(perm_ref[...], u.astype(BF16)) for u in u_blocks]
    ub_f32[1] = _after(ub_f32[1], done_blocks[1])
    ub_blocks = [ub.astype(BF16) for ub in ub_f32]

    def b_proj(j, h):
        s_ref[block_rows[j], 2 * h * HALF_ST:2 * (h + 1) * HALF_ST] = _dot(
            ub_blocks[j][:, h * HALF_IN:(h + 1) * HALF_IN], bd_ref[h])

    def scan_block(j, state):
        out = []
        for (re, im), (sr, si) in zip(cols, state, strict=True):
            lam_re, lam_im = lam_ref[:, re], lam_ref[:, im]
            for l in range(j * PERM_TL, (j + 1) * PERM_TL, BF16_ROWS // SUBLANES):
                pair_re, pair_im = [], []
                for t in range(l, l + BF16_ROWS // SUBLANES):
                    rows = slice(t * SUBLANES, (t + 1) * SUBLANES)
                    sr, si = (lam_re * sr - lam_im * si + s_ref[rows, re],
                              lam_re * si + lam_im * sr + s_ref[rows, im])
                    pair_re.append(sr)
                    pair_im.append(si)
                rows = slice(l * SUBLANES, l * SUBLANES + BF16_ROWS)
                sb_ref[rows, re] = jnp.concatenate(pair_re, axis=0).astype(BF16)
                sb_ref[rows, im] = jnp.concatenate(pair_im, axis=0).astype(BF16)
            out.append((sr, si))
        return out

    def c_proj(j):
        return jnp.concatenate(
            [_dot(sb_ref[block_rows[j], 2 * h * HALF_ST:2 * (h + 1) * HALF_ST], cd_ref[h])
             for h in range(2)], axis=-1)

    assert n_blocks == 2
    state = [(st_ref[:, re], st_ref[:, im]) for re, im in cols]
    b_proj(0, 0)
    gate_conv.append(proj(4 * CONV_CH + half, half))
    b_proj(0, 1)
    gate_ssm = [proj(4 * CONV_CH + D_MODEL, half)]
    state = scan_block(0, state)
    b_proj(1, 0)
    gate_ssm.append(proj(4 * CONV_CH + D_MODEL + half, half))
    b_proj(1, 1)

    v = proj(CONV_CH, CONV_CH) * proj(2 * CONV_CH, CONV_CH)
    v_ref[:, CONV_PAD:CONV_PAD + tl, :] = v.reshape(BATCH, tl, CONV_CH)
    z = cb_ref[...].reshape(1, 1, CONV_CH)
    for k in range(CONV_K):
        lag = CONV_K - 1 - k
        z = z + cw_ref[k:k + 1, :].reshape(1, 1, CONV_CH) * v_ref[
            :, CONV_PAD - lag:CONV_PAD - lag + tl, :]
    v_ref[:, 0:CONV_PAD, :] = v_ref[:, tl:tl + CONV_PAD, :]
    conv_gated = (proj(0, CONV_CH) * z.reshape(tm, CONV_CH)).astype(BF16)
    y_tb_blocks = [c_proj(0)]
    state = scan_block(1, state)
    y_conv = _dot(conv_gated, cwo_ref[...])
    y_tb_blocks.append(c_proj(1))
    for (re, im), (sr, si) in zip(cols, state, strict=True):
        st_ref[:, re] = sr
        st_ref[:, im] = si

    gate_blocks = _time_blocks(jax.nn.sigmoid(jnp.concatenate(gate_ssm, axis=1)), tl)
    merged_blocks = _time_blocks(jax.nn.sigmoid(jnp.concatenate(gate_conv, axis=1)) * y_conv, tl)
    y_blocks = []
    for y_tb, u, gate in zip(y_tb_blocks, u_blocks, gate_blocks, strict=True):
        y_tb = _after(y_tb, gate)
        hi = y_tb.astype(BF16)
        lo = (y_tb - hi.astype(F32)).astype(BF16)
        y_blocks.append(
            _dot(unperm_ref[...], jnp.concatenate([hi, lo], axis=0)) + dsk_ref[...] * u)
    sg_blocks = [jax.nn.gelu(y_blocks[0]).astype(BF16),
                 jax.nn.gelu(_after(y_blocks[1], merged_blocks[1])).astype(BF16)]
    glu_a = [_dot(sg, wglu_ref[:, 0:D_MODEL]) for sg in sg_blocks]
    glu_b = [_dot(sg, wglu_ref[:, D_MODEL:2 * D_MODEL]) for sg in sg_blocks]
    pre_blocks = []
    for j in range(n_blocks):
        m = merged_blocks[j] + gate_blocks[j] * (glu_a[j] * jax.nn.sigmoid(glu_b[j]))
        x = x_ref[:, j * PERM_TL:(j + 1) * PERM_TL, :].reshape(PERM_ROWS, D_MODEL)
        pre_blocks.append(alpha * x + _dot(m.astype(BF16), wout_ref[...]))
    return jnp.concatenate([blk[b * PERM_TL:(b + 1) * PERM_TL]
                            for b in range(BATCH) for blk in pre_blocks], axis=0)


def _resident(shape):
    zeros = (0,) * len(shape)
    return pl.BlockSpec(shape, lambda i: zeros, pipeline_mode=pl.Buffered(1))


def _matmul_tile(n_tiles, first_step=0):
    return lambda i: jnp.clip(i - first_step, 0, n_tiles - 1)


def _finish_tile(first_step=0):
    return lambda i: jnp.maximum(i - first_step - 1, 0)


def _rows(tm, width, tile_of_step):
    return pl.BlockSpec((tm, width), lambda i: (tile_of_step(i), 0))


def _cast_plan(weights, n_tiles, first_step=0):
    specs, shapes = [], []
    tile = _matmul_tile(n_tiles, first_step)
    for w in weights:
        rows, cols = w.shape
        n_blocks = n_tiles
        while (rows // n_blocks) % BF16_ROWS or rows % n_blocks:
            n_blocks //= 2
        repeat = n_tiles // n_blocks
        specs.append(pl.BlockSpec((rows // n_blocks, cols),
                                  lambda i, repeat=repeat: (tile(i) // repeat, 0)))
        shapes.append(jax.ShapeDtypeStruct(w.shape, BF16))
    return specs, shapes


def _ffn_ln(x, win, wo, g, b, next_weights, alpha):
    n_tiles = ROWS // TM_FFN
    first = W_CAST_STEPS
    cast_specs, cast_shapes = _cast_plan(next_weights, n_tiles, first)

    def own_weight(w):
        return pl.BlockSpec((w.shape[0] // first, w.shape[1]),
                            lambda i: (jnp.minimum(i, first - 1), 0))

    return pl.pallas_call(
        functools.partial(_ffn_ln_kernel, alpha, len(next_weights), n_tiles),
        grid=(first + n_tiles + 1,),
        in_specs=[_rows(TM_FFN, D_MODEL, _matmul_tile(n_tiles, first)), own_weight(win),
                  own_weight(wo), _resident(g.shape), _resident(b.shape)] + cast_specs,
        out_specs=[_rows(TM_FFN, D_MODEL, _finish_tile(first))] + cast_specs,
        out_shape=[jax.ShapeDtypeStruct((ROWS, D_MODEL), F32)] + cast_shapes,
        scratch_shapes=[pltpu.VMEM(win.shape, BF16), pltpu.VMEM(wo.shape, BF16),
                        pltpu.VMEM((TM_FFN, D_MODEL), F32)],
        compiler_params=pltpu.CompilerParams(
            dimension_semantics=("arbitrary",), vmem_limit_bytes=VMEM_LIMIT_BYTES),
        name="ffn_ln",
    )(x, win, wo, g, b, *next_weights)


def _ffn_ple(x, p, win, wo, g3, b3, wpi, wpg, g4, b4, alpha):
    n_tiles = ROWS // TM_FFN
    return pl.pallas_call(
        functools.partial(_ffn_ple_kernel, alpha, n_tiles),
        grid=(n_tiles + 1,),
        in_specs=[_rows(TM_FFN, D_MODEL, _matmul_tile(n_tiles)),
                  _rows(TM_FFN, PLE_DIM, _finish_tile()), _resident(win.shape),
                  _resident(wo.shape), _resident(g3.shape),
                  _resident(b3.shape), _resident(wpi.shape), _resident(wpg.shape),
                  _resident(g4.shape), _resident(b4.shape)],
        out_specs=_rows(TM_FFN, D_MODEL, _finish_tile()),
        out_shape=jax.ShapeDtypeStruct((ROWS, D_MODEL), F32),
        scratch_shapes=[pltpu.VMEM((TM_FFN, D_MODEL), F32)],
        compiler_params=pltpu.CompilerParams(
            dimension_semantics=("arbitrary",), vmem_limit_bytes=VMEM_LIMIT_BYTES),
        name="ffn_ple",
    )(x, p, win, wo, g3, b3, wpi, wpg, g4, b4)


def _mixer(x, win, cw, cb, cwo, lam, bd, cd, dsk, perm, unperm, wglu, wout, g, b,
           next_weights, alpha):
    consts = (win, cw, cb, cwo, lam, bd, cd, dsk, perm, unperm, wglu, wout, g, b)
    n_tiles = SEQ // TL_MIX
    cast_specs, cast_shapes = _cast_plan(next_weights, n_tiles)
    matmul_tile, finish_tile = _matmul_tile(n_tiles), _finish_tile()
    block = (BATCH, TL_MIX, D_MODEL)
    return pl.pallas_call(
        functools.partial(_mixer_kernel, alpha, len(next_weights), n_tiles),
        grid=(n_tiles + 1,),
        in_specs=[pl.BlockSpec(block, lambda i: (0, matmul_tile(i), 0))]
        + [_resident(c.shape) for c in consts] + cast_specs,
        out_specs=[pl.BlockSpec(block, lambda i: (0, finish_tile(i), 0))] + cast_specs,
        out_shape=[jax.ShapeDtypeStruct((BATCH, SEQ, D_MODEL), F32)] + cast_shapes,
        scratch_shapes=[
            pltpu.VMEM((TM_MIX, 4 * HALF_ST), F32),
            pltpu.VMEM((TM_MIX, 4 * HALF_ST), BF16),
            pltpu.VMEM((SUBLANES, 4 * HALF_ST), F32),
            pltpu.VMEM((BATCH, CONV_PAD + TL_MIX, CONV_CH), F32),
            pltpu.VMEM((TM_MIX, D_MODEL), F32),
        ],
        compiler_params=pltpu.CompilerParams(
            dimension_semantics=("arbitrary",), vmem_limit_bytes=VMEM_LIMIT_BYTES),
        name="mixer",
    )(x, *consts, *next_weights)


def _ssm_matrices(lam_re, lam_im, log_step, b_re, b_im, c_re, c_im):
    dt = jnp.exp(log_step)[:, None]
    mag = jnp.exp(lam_re * dt)
    lbar_re = mag * jnp.cos(lam_im * dt)
    lbar_im = mag * jnp.sin(lam_im * dt)
    den = lam_re * lam_re + lam_im * lam_im
    q_re = ((lbar_re - 1.0) * lam_re + lbar_im * lam_im) / den
    q_im = (lbar_im * lam_re - (lbar_re - 1.0) * lam_im) / den
    bbar_re = q_re[..., None] * b_re - q_im[..., None] * b_im
    bbar_im = q_re[..., None] * b_im + q_im[..., None] * b_re

    n, w = SSM_STATE, SSM_GROUP
    lam = jnp.stack([lbar_re, lbar_im]).reshape(2, 2, HALF_GROUPS, n)
    lam = jnp.broadcast_to(lam.transpose(1, 0, 2, 3).reshape(1, 4 * HALF_ST),
                           (SUBLANES, 4 * HALF_ST))
    tb = jnp.stack([bbar_re, bbar_im], axis=1).reshape(2, HALF_GROUPS, 2, n, w)
    tb = tb.transpose(0, 1, 4, 2, 3).reshape(2, HALF_IN, 2 * n)
    tc = jnp.stack([c_re, c_im]).reshape(2, 2, HALF_GROUPS, w, n)
    tc = tc.transpose(1, 0, 2, 4, 3).reshape(2, 2 * HALF_ST, w)

    col = np.arange(2 * HALF_ST)
    ch = np.arange(HALF_IN)
    k = np.arange(2 * n)
    spread_b = (k[:, None] // n == col[None, :] // HALF_ST) & (k[:, None] % n == col[None, :] % n)
    own_b = ch[:, None] // w == (col[None, :] % HALF_ST) // n
    spread_c = np.arange(w)[:, None] == ch[None, :] % w
    bd = jnp.einsum("hrk,kc->hrc", tb, jnp.asarray(spread_b, F32)) * jnp.asarray(own_b, F32)
    own_c = own_b.T * np.where(col[:, None] < HALF_ST, 1.0, -1.0)
    cd = jnp.einsum("hrk,kc->hrc", tc, jnp.asarray(spread_c, F32)) * jnp.asarray(own_c, F32)
    return lam, bd.astype(BF16), cd.astype(BF16)


def _row_permutations():
    dst = np.arange(PERM_ROWS)
    src = (dst % BATCH) * PERM_TL + dst // BATCH
    perm = (src[:, None] == np.arange(PERM_ROWS)[None, :]).astype(np.float32)
    return (jnp.asarray(perm, dtype=BF16),
            jnp.asarray(np.concatenate([perm.T, perm.T], axis=1), dtype=BF16))


def kernel(x, p, ffn1_w_in, ffn1_w_out, ln1_g, ln1_b, mix_w_in, conv_w, conv_b, conv_w_out, ssm_lam_re, ssm_lam_im, ssm_log_step, ssm_b_re, ssm_b_im, ssm_c_re, ssm_c_im, ssm_d, ssm_w_glu, mix_w_out, ln2_g, ln2_b, ffn2_w_in, ffn2_w_out, ln3_g, ln3_b, ple_w_in, ple_w_gate, ln4_g, ln4_b):
    depth = p.shape[0]
    alpha = (2.0 * depth) ** 0.25
    assert x.shape == (BATCH, SEQ, D_MODEL) and BATCH == SUBLANES

    def row(v, i):
        return v[i:i + 1].reshape(1, -1)

    perm, unperm = _row_permutations()
    h = x.reshape(ROWS, D_MODEL)
    for i in range(depth):
        h, mw_in, cw_out, w_glu, mw_out = _ffn_ln(
            h, ffn1_w_in[i], ffn1_w_out[i], row(ln1_g, i),
            row(ln1_b, i), (mix_w_in[i], conv_w_out[i], ssm_w_glu[i], mix_w_out[i]), alpha)
        lam, bd, cd = _ssm_matrices(ssm_lam_re[i], ssm_lam_im[i], ssm_log_step[i],
                                    ssm_b_re[i], ssm_b_im[i], ssm_c_re[i], ssm_c_im[i])
        h, f2_in, f2_out, pw_in, pw_gate = _mixer(
            h.reshape(BATCH, SEQ, D_MODEL), mw_in, conv_w[i], row(conv_b, i), cw_out, lam,
            bd, cd, row(ssm_d, i), perm, unperm, w_glu, mw_out, row(ln2_g, i), row(ln2_b, i),
            (ffn2_w_in[i], ffn2_w_out[i], ple_w_in[i], ple_w_gate[i]), alpha)
        h = _ffn_ple(h.reshape(ROWS, D_MODEL), p[i].reshape(ROWS, PLE_DIM), f2_in, f2_out,
                     row(ln3_g, i), row(ln3_b, i), pw_in, pw_gate,
                     row(ln4_g, i), row(ln4_b, i), alpha)
    return h.reshape(BATCH, SEQ, D_MODEL)
```
